```python
import jax, jax.numpy as jnp
from jax import lax
import numpy as np

D_MODEL = 1024
BATCH = 2
SEQ = 16384
DEPTH = 4
DEC_BATCH = 8
DEC_SEQ = 64
PAST_LEN = 4096

CHUNK = 64
N_EVEN = (DEPTH + 1) // 2
N_ODD = DEPTH // 2
H_A = 8
DH_A = 64
W_A = H_A * DH_A
Q_BLOCK = 128
W_B = D_MODEL // 2
K_B = 3
W_C = D_MODEL // 2
G_C = 4
CG_C = W_C // G_C
GMLP_CHUNK = 128
W_D = D_MODEL // 2
K_D = 31
D_FF = 2816
K_FF = 3
EPS = 1e-6
OFF_K = W_A
OFF_V = 2 * W_A
OFF_F = 3 * W_A
OFF_BG = OFF_F + H_A
OFF_CG = OFF_BG + W_B
OFF_X = OFF_CG + W_B
IN_EVEN = OFF_X + W_B
OFF_AD = 2 * W_C
OFF_GD = OFF_AD + W_D
IN_ODD = OFF_GD + W_D

kernel_name = 'hybrid_fox_shortconv_gmlp_conformer_stream_step'


def rmsnorm(x, g):
    x32 = x.astype(jnp.float32)
    y = x32 * lax.rsqrt(jnp.mean(x32 * x32, axis=-1, keepdims=True) + EPS)
    return (y * g.astype(jnp.float32)).astype(x.dtype)


def causal_dwconv(x, hist, w):
    k = w.shape[0]
    xp = jnp.concatenate([hist.astype(x.dtype), x], axis=1)
    y = lax.conv_general_dilated(xp, w[:, None, :].astype(x.dtype), window_strides=(1,), padding='VALID',
                                 dimension_numbers=('NWC', 'WIO', 'NWC'), feature_group_count=x.shape[-1])
    return y, xp[:, xp.shape[1] - (k - 1):]


def fox_attend(q, k, v, cq, ck, q_pos, k_pos):
    s = jnp.einsum('bqhd,bkhd->bhqk', q, k, preferred_element_type=jnp.float32) * (DH_A ** -0.5)
    s = s + jnp.transpose(cq, (0, 2, 1))[..., :, None] - jnp.transpose(ck, (0, 2, 1))[..., None, :]
    s = jnp.where((q_pos[:, None] >= k_pos[None, :])[None, None], s, -jnp.inf)
    p = jax.nn.softmax(s, axis=-1)
    return jnp.einsum('bhqk,bkhd->bqhd', p.astype(v.dtype), v)


def fox_prompt(q, k, v, logf):
    b, t, h, d = q.shape
    c = jnp.cumsum(logf, axis=1)
    pos = jnp.arange(t)
    nb = t // Q_BLOCK
    qb = q.reshape(b, nb, Q_BLOCK, h, d).transpose(1, 0, 2, 3, 4)
    cb = c.reshape(b, nb, Q_BLOCK, h).transpose(1, 0, 2, 3)
    pb = pos.reshape(nb, Q_BLOCK)
    ob = lax.map(lambda a: fox_attend(a[0], k, v, a[1], c, a[2], pos), (qb, cb, pb))
    return ob.transpose(1, 0, 2, 3, 4).reshape(b, t, h, d)


def fox_sample(q, k, v, logf, k_cache, v_cache, logf_cache):
    past = k_cache.shape[1]
    t = q.shape[1]
    kk = jnp.concatenate([k_cache.astype(k.dtype), k], axis=1)
    vv = jnp.concatenate([v_cache.astype(v.dtype), v], axis=1)
    c = jnp.cumsum(jnp.concatenate([logf_cache.astype(jnp.float32), logf], axis=1), axis=1)
    k_pos = jnp.arange(past + t)
    q_pos = past + jnp.arange(t)
    return fox_attend(q, kk, vv, c[:, past:], c, q_pos, k_pos)


def even_mixer(xn, w_in, b_f, g_q, g_k, conv_w, w_out, hist_b, cache):
    b, t, _ = xn.shape
    proj = xn @ w_in
    q = rmsnorm(proj[..., :OFF_K].reshape(b, t, H_A, DH_A), g_q)
    k = rmsnorm(proj[..., OFF_K:OFF_V].reshape(b, t, H_A, DH_A), g_k)
    v = proj[..., OFF_V:OFF_F].reshape(b, t, H_A, DH_A)
    logf = jax.nn.log_sigmoid((proj[..., OFF_F:OFF_BG] + b_f).astype(jnp.float32))
    if cache is None:
        o_a = fox_prompt(q, k, v, logf)
    else:
        o_a = fox_sample(q, k, v, logf, cache[0], cache[1], cache[2])
    bg = proj[..., OFF_BG:OFF_CG]
    cg = proj[..., OFF_CG:OFF_X]
    xin = proj[..., OFF_X:]
    cx, hist = causal_dwconv(cg * xin, hist_b, conv_w)
    o_b = bg * cx
    y = jnp.concatenate([o_a.reshape(b, t, W_A), o_b], axis=-1) @ w_out
    return y, k, v, logf, hist


def spatial_gate(vc, w_s, b_s):
    b, t, _ = vc.shape
    n = -(-t // GMLP_CHUNK)
    tp = n * GMLP_CHUNK
    vp = jnp.pad(vc, ((0, 0), (0, tp - t), (0, 0))).reshape(b, n, GMLP_CHUNK, G_C, CG_C)
    ws = w_s * jnp.tril(jnp.ones((GMLP_CHUNK, GMLP_CHUNK), w_s.dtype))
    s = jnp.einsum('gts,bnsgc->bntgc', ws, vp) + jnp.transpose(b_s)[None, None, :, :, None]
    return s.reshape(b, tp, W_C)[:, :t]


def odd_mixer(xn, w_in, g_vc, w_s, b_s, conv_w, g_d, w_out, hist_d):
    proj = xn @ w_in
    z = jax.nn.gelu(proj[..., :OFF_AD])
    u = z[..., :W_C]
    vc = rmsnorm(z[..., W_C:], g_vc)
    o_c = u * spatial_gate(vc, w_s, b_s)
    glu = proj[..., OFF_AD:OFF_GD] * jax.nn.sigmoid(proj[..., OFF_GD:])
    cd, hist = causal_dwconv(glu, hist_d, conv_w)
    o_d = jax.nn.silu(rmsnorm(cd, g_d))
    y = jnp.concatenate([o_c, o_d], axis=-1) @ w_out
    return y, vc, hist


def conv_ffn(xn, w_up, conv_w, w_down, hist):
    h, hist = causal_dwconv(xn @ w_up, hist, conv_w)
    return (jax.nn.silu(h[..., :D_FF]) * h[..., D_FF:]) @ w_down, hist


def trunk(x, params, hist_b, hist_d, hist_ffn, att_cache):
    (g_mix, w_in_even, b_f, g_q, g_k, conv_b, w_out_even, w_in_odd, g_vc, w_s, b_s,
     conv_d, g_d, w_out_odd, g_ffn, w_up, conv_ffn_w, w_down) = params
    ks, vs, lfs, hbs, vcs, hds, hfs = [], [], [], [], [], [], []
    for l in range(DEPTH):
        i = l // 2
        xn = rmsnorm(x, g_mix[l])
        if l % 2 == 0:
            cache = None if att_cache is None else (att_cache[0][i], att_cache[1][i], att_cache[2][i])
            y, k, v, lf, hb = even_mixer(xn, w_in_even[i], b_f[i], g_q[i], g_k[i], conv_b[i],
                                         w_out_even[i], hist_b[i], cache)
            ks.append(k)
            vs.append(v)
            lfs.append(lf)
            hbs.append(hb)
        else:
            y, vc, hd = odd_mixer(xn, w_in_odd[i], g_vc[i], w_s[i], b_s[i], conv_d[i], g_d[i],
                                  w_out_odd[i], hist_d[i])
            vcs.append(vc)
            hds.append(hd)
        x = x + y
        y, hf = conv_ffn(rmsnorm(x, g_ffn[l]), w_up[l], conv_ffn_w[l], w_down[l], hist_ffn[l])
        x = x + y
        hfs.append(hf)
    return (x, jnp.stack(ks), jnp.stack(vs), jnp.stack(lfs), jnp.stack(hbs),
            jnp.stack(vcs), jnp.stack(hds), jnp.stack(hfs))


def setup_inputs(seed: int = 0) -> dict:
    key = jax.random.key(seed)
    ks = jax.random.split(key, 32)

    def nrm(k, shape, scale):
        return scale * jax.random.normal(k, shape, jnp.float32)

    f_bias = jnp.linspace(1.0, 7.0, H_A, dtype=jnp.float32)
    w_in_even = nrm(ks[9], (N_EVEN, D_MODEL, IN_EVEN), D_MODEL ** -0.5)
    w_in_even = w_in_even.at[:, :, OFF_F:OFF_BG].multiply(0.1)
    return {
        'x_prompt': nrm(ks[0], (BATCH, SEQ, D_MODEL), 1.0),
        'x_sample': nrm(ks[1], (DEC_BATCH, DEC_SEQ, D_MODEL), 1.0),
        'cache_k': nrm(ks[2], (N_EVEN, DEC_BATCH, PAST_LEN, H_A, DH_A), 1.0),
        'cache_v': nrm(ks[3], (N_EVEN, DEC_BATCH, PAST_LEN, H_A, DH_A), 1.0),
        'cache_logf': jax.nn.log_sigmoid(f_bias + nrm(ks[4], (N_EVEN, DEC_BATCH, PAST_LEN, H_A), 0.5)),
        'state_conv_b': nrm(ks[5], (N_EVEN, DEC_BATCH, K_B - 1, W_B), 1.0),
        'state_conv_d': nrm(ks[6], (N_ODD, DEC_BATCH, K_D - 1, W_D), 0.5),
        'state_conv_ffn': nrm(ks[7], (DEPTH, DEC_BATCH, K_FF - 1, 2 * D_FF), 1.0),
        'g_mix': 1.0 + nrm(ks[8], (DEPTH, D_MODEL), 0.01),
        'w_in_even': w_in_even,
        'b_f': f_bias + nrm(ks[10], (N_EVEN, H_A), 0.01),
        'g_q': 1.0 + nrm(ks[11], (N_EVEN, DH_A), 0.01),
        'g_k': 1.0 + nrm(ks[12], (N_EVEN, DH_A), 0.01),
        'conv_b': nrm(ks[13], (N_EVEN, K_B, W_B), K_B ** -0.5),
        'w_out_even': nrm(ks[14], (N_EVEN, W_A + W_B, D_MODEL), 0.5 * (W_A + W_B) ** -0.5),
        'w_in_odd': nrm(ks[15], (N_ODD, D_MODEL, IN_ODD), D_MODEL ** -0.5),
        'g_vc': 1.0 + nrm(ks[16], (N_ODD, W_C), 0.01),
        'w_s': nrm(ks[17], (N_ODD, G_C, GMLP_CHUNK, GMLP_CHUNK), GMLP_CHUNK ** -0.5),
        'b_s': 1.0 + nrm(ks[18], (N_ODD, G_C, GMLP_CHUNK), 0.01),
        'conv_d': nrm(ks[19], (N_ODD, K_D, W_D), K_D ** -0.5),
        'g_d': 1.0 + nrm(ks[20], (N_ODD, W_D), 0.01),
        'w_out_odd': nrm(ks[21], (N_ODD, W_C + W_D, D_MODEL), 0.5 * (W_C + W_D) ** -0.5),
        'g_ffn': 1.0 + nrm(ks[22], (DEPTH, D_MODEL), 0.01),
        'w_up': nrm(ks[23], (DEPTH, D_MODEL, 2 * D_FF), D_MODEL ** -0.5),
        'conv_ffn': nrm(ks[24], (DEPTH, K_FF, 2 * D_FF), K_FF ** -0.5),
        'w_down': nrm(ks[25], (DEPTH, D_FF, D_MODEL), 0.5 * D_FF ** -0.5),
    }


def reference(x_prompt, x_sample, cache_k, cache_v, cache_logf, state_conv_b, state_conv_d, state_conv_ffn,
              g_mix, w_in_even, b_f, g_q, g_k, conv_b, w_out_even, w_in_odd, g_vc, w_s, b_s,
              conv_d, g_d, w_out_odd, g_ffn, w_up, conv_ffn, w_down):
    params = (g_mix, w_in_even, b_f, g_q, g_k, conv_b, w_out_even, w_in_odd, g_vc, w_s, b_s,
              conv_d, g_d, w_out_odd, g_ffn, w_up, conv_ffn, w_down)
    b = x_prompt.shape[0]
    dt = x_prompt.dtype
    zb = jnp.zeros((N_EVEN, b, K_B - 1, W_B), dt)
    zd = jnp.zeros((N_ODD, b, K_D - 1, W_D), dt)
    zf = jnp.zeros((DEPTH, b, K_FF - 1, 2 * D_FF), dt)
    (y_prompt, p_k, p_v, p_logf, p_conv_b, _p_vc, p_conv_d, p_conv_ffn) = trunk(
        x_prompt, params, zb, zd, zf, None)
    (y_sample, s_k, s_v, s_logf, s_conv_b, s_vc, s_conv_d, s_conv_ffn) = trunk(
        x_sample, params, state_conv_b, state_conv_d, state_conv_ffn, (cache_k, cache_v, cache_logf))
    return (y_prompt, y_sample, p_k, p_v, p_logf, p_conv_b, p_conv_d, p_conv_ffn,
            s_k, s_v, s_logf, s_conv_b, s_vc, s_conv_d, s_conv_ffn)
```

```python
import functools
import math

import jax
import jax.numpy as jnp
from jax import lax
from jax.experimental import pallas as pl
from jax.experimental.pallas import tpu as pltpu

F32 = jnp.float32
BF16 = jnp.bfloat16

EPS = 1e-6
LOG2E = math.log2(math.e)
LANES = 128
SUBLANES = 8
N_HEADS = 8
D_HEAD = 64
W_HALF = 512
K_SHORT = 3
K_LONG = 31
GMLP_CHUNK = 128
N_GROUPS = 4
TIME_TILE = 512
ATT_BLOCK = 512
NEG_BIG = -1e30
VMEM_LIMIT = 56 * 1024 * 1024


def _rms(x, g):
    return x * lax.rsqrt(jnp.mean(x * x, axis=-1, keepdims=True) + EPS) * g


def _split3(x):
    hi = x.astype(BF16).astype(F32)
    r = x - hi
    mid = r.astype(BF16).astype(F32)
    lo = r - mid
    return hi, mid, lo


def _tile_cumsum(lf, carry_ref, nb, tt):
    row = lax.broadcasted_iota(jnp.int32, (tt, tt), 0)
    col = lax.broadcasted_iota(jnp.int32, (tt, tt), 1)
    tri = jnp.where(row >= col, 1.0, 0.0).astype(BF16)
    hi, mid, lo = _split3(lf)
    cat = jnp.concatenate([hi, mid, lo], axis=1).astype(BF16)
    outs = []
    for b in range(nb):
        r = jnp.dot(tri, cat[b * tt:(b + 1) * tt], preferred_element_type=F32)
        c_b = r[:, :LANES] + r[:, LANES:2 * LANES] + r[:, 2 * LANES:] + carry_ref[b]
        carry_ref[b] = c_b[tt - 1:tt, :]
        outs.append(c_b)
    return outs[0] if nb == 1 else jnp.concatenate(outs, axis=0)


def _lane_consts():
    lane = lax.broadcasted_iota(jnp.int32, (1, LANES), 1)
    e = lambda i: jnp.where(lane == i, 1.0, 0.0).astype(F32)
    ones_a = e(D_HEAD) + e(D_HEAD + 1) + e(D_HEAD + 2)
    ones_b = e(D_HEAD + 3) + e(D_HEAD + 4) + e(D_HEAD + 5)
    return lane < D_HEAD, e, ones_a, ones_b


def _augment_heads(a, c2, kind, out_ref, nb, tt):
    head_lanes, e, ones_a, ones_b = _lane_consts()
    tm = nb * tt
    for h in range(N_HEADS):
        pair = a[:, LANES * (h // 2):LANES * (h // 2 + 1)]
        if h % 2:
            pair = pltpu.roll(pair, D_HEAD, axis=1)
        if kind == 'v':
            ext = e(D_HEAD)
        else:
            cb = jnp.broadcast_to(c2[:, h:h + 1], (tm, LANES))
            hi, mid, lo = _split3(cb)
            if kind == 'q':
                ext = hi * e(D_HEAD) + mid * e(D_HEAD + 1) + lo * e(D_HEAD + 2) + ones_b
            else:
                ext = ones_a - (hi * e(D_HEAD + 3) + mid * e(D_HEAD + 4) + lo * e(D_HEAD + 5))
        out_ref[:, h] = jnp.where(head_lanes, pair, ext).astype(BF16).reshape(nb, tt, LANES)


def _log_sigmoid(z):
    return -(jnp.maximum(-z, 0.0) + jnp.log1p(jnp.exp(-jnp.abs(z))))


def _masked_logf(z):
    lane = lax.broadcasted_iota(jnp.int32, (1, LANES), 1)
    return jnp.where(lane < N_HEADS, _log_sigmoid(z), 0.0)


def _even_in_kernel(x_ref, hist_ref, c0_ref, gmix_ref, w_ref, wf_ref, bf_ref, gq_ref, gk_ref, cw_ref, hm_ref,
                    qa_ref, ka_ref, va_ref, k_ref, v_ref, lf_ref, ob_ref, hout_ref,
                    cbuf, ccar, *, nb, tt):
    tm = nb * tt
    pad = SUBLANES

    @pl.when(pl.program_id(1) == 0)
    def _():
        cbuf[:, pad - (K_SHORT - 1):pad, :] = hist_ref[...]
        ccar[...] = c0_ref[...]

    x = x_ref[...].reshape(tm, x_ref.shape[-1])
    xn = _rms(x, gmix_ref[...]).astype(BF16)

    def proj(i):
        return jnp.dot(xn, w_ref[:, i * W_HALF:(i + 1) * W_HALF], preferred_element_type=F32)

    hm = hm_ref[...]

    def headnorm(a, g):
        ms = jnp.dot((a * a).astype(BF16), hm, preferred_element_type=F32)
        return a * lax.rsqrt(ms + EPS) * g

    qn = headnorm(proj(0), gq_ref[...])
    kn = headnorm(proj(1), gk_ref[...])
    v = proj(2)
    k_ref[...] = kn.reshape(nb, tt, W_HALF)
    v_ref[...] = v.reshape(nb, tt, W_HALF)

    z = jnp.dot(xn, wf_ref[...], preferred_element_type=F32) + bf_ref[...]
    lf = _masked_logf(z)
    lf_ref[...] = lf[:, :N_HEADS].reshape(nb, tt, N_HEADS)
    c2 = _tile_cumsum(lf, ccar, nb, tt) * LOG2E

    _augment_heads(qn * (LOG2E * D_HEAD ** -0.5), c2, 'q', qa_ref, nb, tt)
    _augment_heads(kn, c2, 'k', ka_ref, nb, tt)
    _augment_heads(v, None, 'v', va_ref, nb, tt)

    bg = proj(3)
    cgx = proj(4) * proj(5)
    cbuf[:, pad:pad + tt, :] = cgx.reshape(nb, tt, W_HALF)
    cw = cw_ref[...]
    cx = (cw[2:3] * cbuf[:, pad:pad + tt, :] + cw[1:2] * cbuf[:, pad - 1:pad - 1 + tt, :]
          + cw[0:1] * cbuf[:, pad - 2:pad - 2 + tt, :])
    ob_ref[...] = (bg.reshape(nb, tt, W_HALF) * cx).astype(BF16)
    tail = cbuf[:, pad + tt - (K_SHORT - 1):pad + tt, :]
    cbuf[:, pad - (K_SHORT - 1):pad, :] = tail
    hout_ref[...] = tail


def _const_spec(shape, single=True):
    idx = lambda *_: (0,) * len(shape)
    if single:
        return pl.BlockSpec(shape, idx, pipeline_mode=pl.Buffered(1))
    return pl.BlockSpec(shape, idx)


def _tiles(b, t):
    if t >= TIME_TILE:
        assert t % TIME_TILE == 0
        return 1, TIME_TILE
    assert t % (2 * SUBLANES) == 0
    return b, t


def _even_in(x, hist_b, c0, gmix, w_main, w_f, b_f, gq, gk, conv_w, hmat):
    b, t, d = x.shape
    assert t >= K_SHORT - 1
    nb, tt = _tiles(b, t)
    grid = (b // nb, t // tt)
    row = lambda shape: pl.BlockSpec(shape, lambda i, j: (i, j, 0))
    per_b = lambda shape: pl.BlockSpec(shape, lambda i, j: (i, 0, 0))
    heads = pl.BlockSpec((nb, N_HEADS, tt, LANES), lambda i, j: (i, 0, j, 0))
    out_shape = (
        jax.ShapeDtypeStruct((b, N_HEADS, t, LANES), BF16),
        jax.ShapeDtypeStruct((b, N_HEADS, t, LANES), BF16),
        jax.ShapeDtypeStruct((b, N_HEADS, t, LANES), BF16),
        jax.ShapeDtypeStruct((b, t, W_HALF), F32),
        jax.ShapeDtypeStruct((b, t, W_HALF), F32),
        jax.ShapeDtypeStruct((b, t, N_HEADS), F32),
        jax.ShapeDtypeStruct((b, t, W_HALF), BF16),
        jax.ShapeDtypeStruct((b, K_SHORT - 1, W_HALF), F32),
    )
    return pl.pallas_call(
        functools.partial(_even_in_kernel, nb=nb, tt=tt),
        grid=grid,
        in_specs=[row((nb, tt, d)), per_b((nb, K_SHORT - 1, W_HALF)), per_b((nb, 1, LANES)),
                  _const_spec(gmix.shape), _const_spec(w_main.shape), _const_spec(w_f.shape),
                  _const_spec(b_f.shape), _const_spec(gq.shape), _const_spec(gk.shape),
                  _const_spec(conv_w.shape), _const_spec(hmat.shape)],
        out_specs=(heads, heads, heads, row((nb, tt, W_HALF)), row((nb, tt, W_HALF)), row((nb, tt, N_HEADS)),
                   row((nb, tt, W_HALF)), per_b((nb, K_SHORT - 1, W_HALF))),
        out_shape=out_shape,
        scratch_shapes=[pltpu.VMEM((nb, tt + SUBLANES, W_HALF), F32), pltpu.VMEM((nb, 1, LANES), F32)],
        compiler_params=pltpu.CompilerParams(dimension_semantics=("arbitrary", "arbitrary"),
                                             vmem_limit_bytes=VMEM_LIMIT),
        name="even_in",
    )(x, hist_b, c0, gmix, w_main, w_f, b_f, gq, gk, conv_w, hmat)


def _cache_prep_kernel(k_ref, v_ref, lf_ref, ka_ref, va_ref, ctot_ref, ccar, *, tt):
    @pl.when(pl.program_id(1) == 0)
    def _():
        ccar[...] = jnp.zeros_like(ccar)

    c2 = _tile_cumsum(lf_ref[0], ccar, 1, tt) * LOG2E
    _augment_heads(k_ref[0], c2, 'k', ka_ref, 1, tt)
    _augment_heads(v_ref[0], None, 'v', va_ref, 1, tt)
    ctot_ref[...] = ccar[...]


def _cache_prep(k, v, lf_pad):
    b, t, _ = k.shape
    tt = min(t, TIME_TILE)
    assert t % tt == 0
    row = lambda w: pl.BlockSpec((1, tt, w), lambda i, j: (i, j, 0))
    heads = pl.BlockSpec((1, N_HEADS, tt, LANES), lambda i, j: (i, 0, j, 0))
    return pl.pallas_call(
        functools.partial(_cache_prep_kernel, tt=tt),
        grid=(b, t // tt),
        in_specs=[row(W_HALF), row(W_HALF), row(LANES)],
        out_specs=(heads, heads, pl.BlockSpec((1, 1, LANES), lambda i, j: (i, 0, 0))),
        out_shape=(jax.ShapeDtypeStruct((b, N_HEADS, t, LANES), BF16),
                   jax.ShapeDtypeStruct((b, N_HEADS, t, LANES), BF16),
                   jax.ShapeDtypeStruct((b, 1, LANES), F32)),
        scratch_shapes=[pltpu.VMEM((1, 1, LANES), F32)],
        compiler_params=pltpu.CompilerParams(dimension_semantics=("arbitrary", "arbitrary"),
                                             vmem_limit_bytes=VMEM_LIMIT),
        name="cache_prep",
    )(k, v, lf_pad)


def _attn_kernel(qa_ref, ka_ref, va_ref, o_ref, *, tq, tk, past):
    i = pl.program_id(2)
    diag_off = past + i * tq
    n_full = diag_off // tk
    nt = (((1,), (1,)), ((), ()))
    qs = [qa_ref[0, hh] for hh in range(2)]

    def update(state, hh, kb, vb, mask):
        m, acc = state
        s = lax.dot_general(qs[hh], kb, nt, preferred_element_type=F32)
        if mask is not None:
            s = jnp.where(mask, s, NEG_BIG)
        m_new = jnp.maximum(m, jnp.max(s, axis=-1, keepdims=True))
        p = jnp.exp2(s - m_new)
        acc = jnp.exp2(m - m_new) * acc + jnp.dot(p.astype(BF16), vb, preferred_element_type=F32)
        return m_new, acc

    def body(j, states):
        off = pl.multiple_of(j * tk, tk)
        return tuple(update(states[hh], hh, ka_ref[0, hh, pl.ds(off, tk), :], va_ref[0, hh, pl.ds(off, tk), :], None)
                     for hh in range(2))

    init = tuple((jnp.full((tq, 1), NEG_BIG, F32), jnp.zeros((tq, LANES), F32)) for _ in range(2))
    states = lax.fori_loop(0, n_full, body, init)

    doff = pl.multiple_of(diag_off, tq)
    mask = lax.broadcasted_iota(jnp.int32, (tq, tq), 0) >= lax.broadcasted_iota(jnp.int32, (tq, tq), 1)
    outs = []
    for hh in range(2):
        _, acc = update(states[hh], hh, ka_ref[0, hh, pl.ds(doff, tq), :], va_ref[0, hh, pl.ds(doff, tq), :], mask)
        outs.append(acc / acc[:, D_HEAD:D_HEAD + 1])
    lane = lax.broadcasted_iota(jnp.int32, (1, LANES), 1)
    o_ref[0] = jnp.where(lane < D_HEAD, outs[0], pltpu.roll(outs[1], D_HEAD, axis=1)).astype(BF16)


def _attention(qa, ka, va):
    b, h, t_q, _ = qa.shape
    t_k = ka.shape[2]
    past = t_k - t_q
    tq = min(ATT_BLOCK, t_q)
    tk = tq if past == 0 else math.gcd(past, ATT_BLOCK)
    assert t_q % tq == 0 and past % tk == 0 and (tq % tk == 0 or t_q == tq)
    return pl.pallas_call(
        functools.partial(_attn_kernel, tq=tq, tk=tk, past=past),
        grid=(b, h // 2, t_q // tq),
        in_specs=[pl.BlockSpec((1, 2, tq, LANES), lambda i, p, j: (i, p, j, 0)),
                  pl.BlockSpec((1, 2, t_k, LANES), lambda i, p, j: (i, p, 0, 0)),
                  pl.BlockSpec((1, 2, t_k, LANES), lambda i, p, j: (i, p, 0, 0))],
        out_specs=pl.BlockSpec((1, tq, LANES), lambda i, p, j: (i, j, p)),
        out_shape=jax.ShapeDtypeStruct((b, t_q, h * D_HEAD), BF16),
        compiler_params=pltpu.CompilerParams(dimension_semantics=("arbitrary", "arbitrary", "arbitrary"),
                                             vmem_limit_bytes=VMEM_LIMIT),
        name="fox_attention",
    )(qa, ka, va)


def _gelu_tanh(x):
    return 0.5 * x * (1.0 + jnp.tanh(math.sqrt(2.0 / math.pi) * (x + 0.044715 * (x * x * x))))


def _odd_in_kernel(x_ref, hist_ref, gmix_ref, w_ref, gvc_ref, ws_ref, bs_ref, cw_ref, gd_ref,
                   oc_ref, od_ref, hout_ref, vc_ref, dbuf, *, nb, tt, cs):
    tm = nb * tt
    pad = 4 * SUBLANES
    nh = K_LONG - 1

    @pl.when(pl.program_id(1) == 0)
    def _():
        dbuf[:, pad - nh:pad, :] = hist_ref[...]

    x = x_ref[...].reshape(tm, x_ref.shape[-1])
    xn = _rms(x, gmix_ref[...]).astype(BF16)

    def proj(i):
        return jnp.dot(xn, w_ref[:, i * W_HALF:(i + 1) * W_HALF], preferred_element_type=F32)

    u = _gelu_tanh(proj(0))
    vc = _rms(_gelu_tanh(proj(1)), gvc_ref[...])
    vc_ref[...] = vc.reshape(nb, tt, W_HALF)
    vcb = vc.astype(BF16)

    row = lax.broadcasted_iota(jnp.int32, (cs, cs), 0)
    col = lax.broadcasted_iota(jnp.int32, (cs, cs), 1)
    per_b = tt // cs
    for g in range(N_GROUPS):
        wsg = jnp.where(row >= col, ws_ref[g], 0.0).astype(BF16)
        for ci in range(tm // cs):
            r0 = ci * cs
            gate = jnp.dot(wsg, vcb[r0:r0 + cs, g * LANES:(g + 1) * LANES], preferred_element_type=F32) + bs_ref[g]
            oc = u[r0:r0 + cs, g * LANES:(g + 1) * LANES] * gate
            t0 = (ci % per_b) * cs
            oc_ref[ci // per_b, t0:t0 + cs, g * LANES:(g + 1) * LANES] = oc.astype(BF16)

    glu = proj(2) * jax.nn.sigmoid(proj(3))
    dbuf[:, pad:pad + tt, :] = glu.reshape(nb, tt, W_HALF)
    cw = cw_ref[...]
    cd = cw[0:1] * dbuf[:, pad - nh:pad - nh + tt, :]
    for k in range(1, K_LONG):
        cd = cd + cw[k:k + 1] * dbuf[:, pad - nh + k:pad - nh + k + tt, :]
    y = _rms(cd.reshape(tm, W_HALF), gd_ref[...])
    od_ref[...] = (y * jax.nn.sigmoid(y)).astype(BF16).reshape(nb, tt, W_HALF)
    tail = dbuf[:, pad + tt - nh:pad + tt, :]
    dbuf[:, pad - nh:pad, :] = tail
    hout_ref[...] = tail


def _odd_in(x, hist_d, gmix, w_in, gvc, ws, bs, conv_w, gd):
    b, t, d = x.shape
    assert t >= K_LONG - 1
    nb, tt = _tiles(b, t)
    cs = min(GMLP_CHUNK, tt)
    assert tt % cs == 0
    grid = (b // nb, t // tt)
    row = lambda shape: pl.BlockSpec(shape, lambda i, j: (i, j, 0))
    per_b = lambda shape: pl.BlockSpec(shape, lambda i, j: (i, 0, 0))
    ws = ws[:, :cs, :cs]
    bs = bs[:, :cs, :]
    return pl.pallas_call(
        functools.partial(_odd_in_kernel, nb=nb, tt=tt, cs=cs),
        grid=grid,
        in_specs=[row((nb, tt, d)), per_b((nb, K_LONG - 1, W_HALF)),
                  _const_spec(gmix.shape), _const_spec(w_in.shape), _const_spec(gvc.shape),
                  _const_spec(ws.shape), _const_spec(bs.shape), _const_spec(conv_w.shape), _const_spec(gd.shape)],
        out_specs=(row((nb, tt, W_HALF)), row((nb, tt, W_HALF)), per_b((nb, K_LONG - 1, W_HALF)),
                   row((nb, tt, W_HALF))),
        out_shape=(jax.ShapeDtypeStruct((b, t, W_HALF), BF16),
                   jax.ShapeDtypeStruct((b, t, W_HALF), BF16),
                   jax.ShapeDtypeStruct((b, K_LONG - 1, W_HALF), F32),
                   jax.ShapeDtypeStruct((b, t, W_HALF), F32)),
        scratch_shapes=[pltpu.VMEM((nb, tt + 4 * SUBLANES, W_HALF), F32)],
        compiler_params=pltpu.CompilerParams(dimension_semantics=("arbitrary", "arbitrary"),
                                             vmem_limit_bytes=VMEM_LIMIT),
        name="odd_in",
    )(x, hist_d, gmix, w_in, gvc, ws, bs, conv_w, gd)


def _mix_ffn_kernel(x_ref, oa_ref, ob_ref, hist_ref, wo_ref, g_ref, wu_ref, cw_ref, wd_ref,
                    y_ref, hout_ref, hbuf, abuf, *, nb, tt, n_chunks):
    tm = nb * tt
    pad = SUBLANES
    d_ff = wd_ref.shape[0]
    ch = d_ff // n_chunks

    @pl.when(pl.program_id(1) == 0)
    def _():
        hbuf[:, pad - (K_SHORT - 1):pad, :] = hist_ref[...]

    d = x_ref.shape[-1]
    oa = oa_ref[...].reshape(tm, W_HALF)
    ob = ob_ref[...].reshape(tm, W_HALF)
    x1 = (x_ref[...].reshape(tm, d)
          + jnp.dot(oa, wo_ref[:W_HALF, :], preferred_element_type=F32)
          + jnp.dot(ob, wo_ref[W_HALF:, :], preferred_element_type=F32))
    xn = _rms(x1, g_ref[...]).astype(BF16)

    def conv_up(lo):
        h = jnp.dot(xn, wu_ref[:, lo:lo + ch], preferred_element_type=F32)
        hbuf[:, pad:pad + tt, lo:lo + ch] = h.reshape(nb, tt, ch)
        cw = cw_ref[:, lo:lo + ch]
        return (cw[2:3] * hbuf[:, pad:pad + tt, lo:lo + ch]
                + cw[1:2] * hbuf[:, pad - 1:pad - 1 + tt, lo:lo + ch]
                + cw[0:1] * hbuf[:, pad - 2:pad - 2 + tt, lo:lo + ch]).reshape(tm, ch)

    for c in range(n_chunks):
        gate = conv_up(c * ch)
        val = conv_up(d_ff + c * ch)
        abuf[:, c * ch:(c + 1) * ch] = (gate * jax.nn.sigmoid(gate) * val).astype(BF16)
    y_ref[...] = (x1 + jnp.dot(abuf[...], wd_ref[...], preferred_element_type=F32)).reshape(nb, tt, d)
    tail = hbuf[:, pad + tt - (K_SHORT - 1):pad + tt, :]
    hbuf[:, pad - (K_SHORT - 1):pad, :] = tail
    hout_ref[...] = tail


def _mix_ffn(x, oa, ob, hist_ffn, w_out, g_ffn, w_up, conv_w, w_down, n_chunks=11):
    b, t, d = x.shape
    assert t >= K_SHORT - 1
    nb, tt = _tiles(b, t)
    grid = (b // nb, t // tt)
    two_ff = w_up.shape[1]
    row = lambda shape: pl.BlockSpec(shape, lambda i, j: (i, j, 0))
    per_b = lambda shape: pl.BlockSpec(shape, lambda i, j: (i, 0, 0))
    return pl.pallas_call(
        functools.partial(_mix_ffn_kernel, nb=nb, tt=tt, n_chunks=n_chunks),
        grid=grid,
        in_specs=[row((nb, tt, d)), row((nb, tt, W_HALF)), row((nb, tt, W_HALF)), per_b((nb, K_SHORT - 1, two_ff)),
                  _const_spec(w_out.shape), _const_spec(g_ffn.shape), _const_spec(w_up.shape),
                  _const_spec(conv_w.shape), _const_spec(w_down.shape)],
        out_specs=(row((nb, tt, d)), per_b((nb, K_SHORT - 1, two_ff))),
        out_shape=(jax.ShapeDtypeStruct((b, t, d), F32),
                   jax.ShapeDtypeStruct((b, K_SHORT - 1, two_ff), F32)),
        scratch_shapes=[pltpu.VMEM((nb, tt + SUBLANES, two_ff), F32), pltpu.VMEM((nb * tt, two_ff // 2), BF16)],
        compiler_params=pltpu.CompilerParams(dimension_semantics=("arbitrary", "arbitrary"),
                                             vmem_limit_bytes=VMEM_LIMIT),
        name="mix_ffn",
    )(x, oa, ob, hist_ffn, w_out, g_ffn, w_up, conv_w, w_down)


def _prep_params(g_mix, w_in_even, b_f, g_q, g_k, conv_b, w_out_even, w_in_odd, g_vc, w_s, b_s,
                 conv_d, g_d, w_out_odd, g_ffn, w_up, conv_ffn, w_down):
    depth = g_mix.shape[0]
    off_f = 3 * W_HALF
    head_id = jnp.arange(W_HALF) // D_HEAD
    hmat = jnp.where(head_id[:, None] == head_id[None, :], 1.0 / D_HEAD, 0.0).astype(BF16)
    layers = []
    for l in range(depth):
        i = l // 2
        p = dict(g_mix=g_mix[l][None], g_ffn=g_ffn[l][None], w_up=w_up[l].astype(BF16),
                 conv_ffn=conv_ffn[l], w_down=w_down[l].astype(BF16))
        if l % 2 == 0:
            w = w_in_even[i]
            p.update(
                w_main=jnp.concatenate([w[:, :off_f], w[:, off_f + N_HEADS:]], axis=1).astype(BF16),
                w_f=jnp.pad(w[:, off_f:off_f + N_HEADS], ((0, 0), (0, LANES - N_HEADS))).astype(BF16),
                b_f=jnp.pad(b_f[i], (0, LANES - N_HEADS))[None],
                g_q=jnp.tile(g_q[i], N_HEADS)[None], g_k=jnp.tile(g_k[i], N_HEADS)[None],
                conv_w=conv_b[i], w_out=w_out_even[i].astype(BF16), hmat=hmat)
        else:
            p.update(
                w_in=w_in_odd[i].astype(BF16), g_vc=g_vc[i][None], w_s=w_s[i],
                b_s=jnp.broadcast_to(b_s[i][:, :, None], b_s[i].shape + (LANES,)),
                conv_w=conv_d[i], g_d=g_d[i][None], w_out=w_out_odd[i].astype(BF16))
        layers.append(p)
    return layers


def _trunk(x, layers, hist_b, hist_d, hist_ffn, att_cache):
    b = x.shape[0]
    ks, vs, lfs, hbs, vcs, hds, hfs = [], [], [], [], [], [], []
    for l, p in enumerate(layers):
        i = l // 2
        if l % 2 == 0:
            if att_cache is None:
                c0 = jnp.zeros((b, 1, LANES), F32)
            else:
                ck, cv, clf = att_cache[0][i], att_cache[1][i], att_cache[2][i]
                past = ck.shape[1]
                ka_c, va_c, c0 = _cache_prep(ck.reshape(b, past, W_HALF), cv.reshape(b, past, W_HALF),
                                             jnp.pad(clf, ((0, 0), (0, 0), (0, LANES - N_HEADS))))
            qa, ka, va, k, v, lf, ob, hb = _even_in(x, hist_b[i], c0, p['g_mix'], p['w_main'], p['w_f'], p['b_f'],
                                                    p['g_q'], p['g_k'], p['conv_w'], p['hmat'])
            if att_cache is not None:
                ka = jnp.concatenate([ka_c, ka], axis=2)
                va = jnp.concatenate([va_c, va], axis=2)
            oa = _attention(qa, ka, va)
            t = x.shape[1]
            ks.append(k.reshape(b, t, N_HEADS, D_HEAD))
            vs.append(v.reshape(b, t, N_HEADS, D_HEAD))
            lfs.append(lf)
            hbs.append(hb)
        else:
            oa, ob, hd, vc = _odd_in(x, hist_d[i], p['g_mix'], p['w_in'], p['g_vc'], p['w_s'], p['b_s'],
                                     p['conv_w'], p['g_d'])
            vcs.append(vc)
            hds.append(hd)
        x, hf = _mix_ffn(x, oa, ob, hist_ffn[l], p['w_out'], p['g_ffn'], p['w_up'], p['conv_ffn'], p['w_down'])
        hfs.append(hf)
    return (x, jnp.stack(ks), jnp.stack(vs), jnp.stack(lfs), jnp.stack(hbs),
            jnp.stack(vcs), jnp.stack(hds), jnp.stack(hfs))


def kernel(x_prompt, x_sample, cache_k, cache_v, cache_logf, state_conv_b, state_conv_d, state_conv_ffn,
           g_mix, w_in_even, b_f, g_q, g_k, conv_b, w_out_even, w_in_odd, g_vc, w_s, b_s,
           conv_d, g_d, w_out_odd, g_ffn, w_up, conv_ffn, w_down):
    layers = _prep_params(g_mix, w_in_even, b_f, g_q, g_k, conv_b, w_out_even, w_in_odd, g_vc, w_s, b_s,
                          conv_d, g_d, w_out_odd, g_ffn, w_up, conv_ffn, w_down)
    b = x_prompt.shape[0]
    depth = g_mix.shape[0]
    n_even, n_odd = (depth + 1) // 2, depth // 2
    zb = jnp.zeros((n_even, b, K_SHORT - 1, W_HALF), F32)
    zd = jnp.zeros((n_odd, b, K_LONG - 1, W_HALF), F32)
    zf = jnp.zeros((depth, b, K_SHORT - 1, w_up.shape[-1]), F32)
    (y_prompt, p_k, p_v, p_logf, p_conv_b, _, p_conv_d, p_conv_ffn) = _trunk(
        x_prompt, layers, zb, zd, zf, None)
    (y_sample, s_k, s_v, s_logf, s_conv_b, s_vc, s_conv_d, s_conv_ffn) = _trunk(
        x_sample, layers, state_conv_b, state_conv_d, state_conv_ffn, (cache_k, cache_v, cache_logf))
    return (y_prompt, y_sample, p_k, p_v, p_logf, p_conv_b, p_conv_d, p_conv_ffn,
            s_k, s_v, s_logf, s_conv_b, s_vc, s_conv_d, s_conv_ffn)
```

```python
import functools
import math

import jax
import jax.numpy as jnp
from jax import lax
from jax.experimental import pallas as pl
from jax.experimental.pallas import tpu as pltpu

F32 = jnp.float32
BF16 = jnp.bfloat16

EPS = 1e-6
LOG2E = math.log2(math.e)
LANES = 128
SUBLANES = 8
N_HEADS = 8
D_HEAD = 64
W_HALF = 512
K_SHORT = 3
K_LONG = 31
GMLP_CHUNK = 128
N_GROUPS = 4
TIME_TILE = 512
ATT_BLOCK = 1024
ATT_CACHE_BLOCK = 4096
NEG_BIG = -1e30
CONV_ACC_ROWS = 128
VMEM_LIMIT = 56 * 1024 * 1024


def _rms(x, g):
    return x * lax.rsqrt(jnp.mean(x * x, axis=-1, keepdims=True) + EPS) * g


def _split3(x):
    hi = x.astype(BF16).astype(F32)
    r = x - hi
    mid = r.astype(BF16).astype(F32)
    lo = r - mid
    return hi, mid, lo


def _tile_cumsum(lf, carry_ref, nb, tt):
    row = lax.broadcasted_iota(jnp.int32, (tt, tt), 0)
    col = lax.broadcasted_iota(jnp.int32, (tt, tt), 1)
    tri = jnp.where(row >= col, 1.0, 0.0).astype(BF16)
    hi, mid, lo = _split3(lf)
    cat = jnp.concatenate([hi, mid, lo], axis=1).astype(BF16)
    outs = []
    for b in range(nb):
        r = jnp.dot(tri, cat[b * tt:(b + 1) * tt], preferred_element_type=F32)
        c_b = r[:, :LANES] + r[:, LANES:2 * LANES] + r[:, 2 * LANES:] + carry_ref[b]
        carry_ref[b] = c_b[tt - 1:tt, :]
        outs.append(c_b)
    return outs[0] if nb == 1 else jnp.concatenate(outs, axis=0)


def _lane_consts():
    lane = lax.broadcasted_iota(jnp.int32, (1, LANES), 1)
    e = lambda i: jnp.where(lane == i, 1.0, 0.0).astype(F32)
    ones_a = e(D_HEAD) + e(D_HEAD + 1) + e(D_HEAD + 2)
    ones_b = e(D_HEAD + 3) + e(D_HEAD + 4) + e(D_HEAD + 5)
    return lane < D_HEAD, e, ones_a, ones_b


def _augment_heads(a, c2, kind, out_ref, nb, tt):
    head_lanes, e, ones_a, ones_b = _lane_consts()
    tm = nb * tt
    for h in range(N_HEADS):
        pair = a[:, LANES * (h // 2):LANES * (h // 2 + 1)]
        if h % 2:
            pair = pltpu.roll(pair, D_HEAD, axis=1)
        if kind == 'v':
            ext = e(D_HEAD)
        else:
            cb = jnp.broadcast_to(c2[:, h:h + 1], (tm, LANES))
            hi, mid, lo = _split3(cb)
            if kind == 'q':
                ext = hi * e(D_HEAD) + mid * e(D_HEAD + 1) + lo * e(D_HEAD + 2) + ones_b
            else:
                ext = ones_a - (hi * e(D_HEAD + 3) + mid * e(D_HEAD + 4) + lo * e(D_HEAD + 5))
        out_ref[:, h] = jnp.where(head_lanes, pair, ext).astype(BF16).reshape(nb, tt, LANES)


def _log_sigmoid(z):
    return -(jnp.maximum(-z, 0.0) + jnp.log1p(jnp.exp(-jnp.abs(z))))


def _masked_logf(z):
    lane = lax.broadcasted_iota(jnp.int32, (1, LANES), 1)
    return jnp.where(lane < N_HEADS, _log_sigmoid(z), 0.0)


def _even_in_kernel(x_ref, hist_ref, c0_ref, gmix_ref, w_ref, wf_ref, bf_ref, gq_ref, gk_ref, cw_ref, hm_ref,
                    qa_ref, ka_ref, va_ref, k_ref, v_ref, lf_ref, ob_ref, hout_ref,
                    cbuf, ccar, *, nb, tt):
    tm = nb * tt
    pad = SUBLANES

    @pl.when(pl.program_id(1) == 0)
    def _():
        cbuf[:, pad - (K_SHORT - 1):pad, :] = hist_ref[...]
        ccar[...] = c0_ref[...]

    x = x_ref[...].reshape(tm, x_ref.shape[-1])
    xn = _rms(x, gmix_ref[...]).astype(BF16)

    def proj(i):
        return jnp.dot(xn, w_ref[:, i * W_HALF:(i + 1) * W_HALF], preferred_element_type=F32)

    hm = hm_ref[...]

    def headnorm(a, g):
        ms = jnp.dot((a * a).astype(BF16), hm, preferred_element_type=F32)
        return a * lax.rsqrt(ms + EPS) * g

    qn = headnorm(proj(0), gq_ref[...])
    kn = headnorm(proj(1), gk_ref[...])
    v = proj(2)
    k_ref[...] = kn.reshape(nb, tt, W_HALF)
    v_ref[...] = v.reshape(nb, tt, W_HALF)

    z = jnp.dot(xn, wf_ref[...], preferred_element_type=F32) + bf_ref[...]
    lf = _masked_logf(z)
    lf_ref[...] = lf[:, :N_HEADS].reshape(nb, tt, N_HEADS)
    c2 = _tile_cumsum(lf, ccar, nb, tt) * LOG2E

    _augment_heads(qn * (LOG2E * D_HEAD ** -0.5), c2, 'q', qa_ref, nb, tt)
    _augment_heads(kn, c2, 'k', ka_ref, nb, tt)
    _augment_heads(v, None, 'v', va_ref, nb, tt)

    bg = proj(3)
    cgx = proj(4) * proj(5)
    cbuf[:, pad:pad + tt, :] = cgx.reshape(nb, tt, W_HALF)
    cw = cw_ref[...]
    cx = (cw[2:3] * cbuf[:, pad:pad + tt, :] + cw[1:2] * cbuf[:, pad - 1:pad - 1 + tt, :]
          + cw[0:1] * cbuf[:, pad - 2:pad - 2 + tt, :])
    ob_ref[...] = (bg.reshape(nb, tt, W_HALF) * cx).astype(BF16)
    tail = cbuf[:, pad + tt - (K_SHORT - 1):pad + tt, :]
    cbuf[:, pad - (K_SHORT - 1):pad, :] = tail
    hout_ref[...] = tail


def _const_spec(shape, single=True):
    idx = lambda *_: (0,) * len(shape)
    if single:
        return pl.BlockSpec(shape, idx, pipeline_mode=pl.Buffered(1))
    return pl.BlockSpec(shape, idx)


def _tiles(b, t):
    if t >= TIME_TILE:
        assert t % TIME_TILE == 0
        return 1, TIME_TILE
    assert t % (2 * SUBLANES) == 0
    return b, t


def _even_in(x, hist_b, c0, gmix, w_main, w_f, b_f, gq, gk, conv_w, hmat):
    b, t, d = x.shape
    assert t >= K_SHORT - 1
    nb, tt = _tiles(b, t)
    grid = (b // nb, t // tt)
    row = lambda shape: pl.BlockSpec(shape, lambda i, j: (i, j, 0))
    per_b = lambda shape: pl.BlockSpec(shape, lambda i, j: (i, 0, 0))
    heads = pl.BlockSpec((nb, N_HEADS, tt, LANES), lambda i, j: (i, 0, j, 0))
    out_shape = (
        jax.ShapeDtypeStruct((b, N_HEADS, t, LANES), BF16),
        jax.ShapeDtypeStruct((b, N_HEADS, t, LANES), BF16),
        jax.ShapeDtypeStruct((b, N_HEADS, t, LANES), BF16),
        jax.ShapeDtypeStruct((b, t, W_HALF), F32),
        jax.ShapeDtypeStruct((b, t, W_HALF), F32),
        jax.ShapeDtypeStruct((b, t, N_HEADS), F32),
        jax.ShapeDtypeStruct((b, t, W_HALF), BF16),
        jax.ShapeDtypeStruct((b, K_SHORT - 1, W_HALF), F32),
    )
    return pl.pallas_call(
        functools.partial(_even_in_kernel, nb=nb, tt=tt),
        grid=grid,
        in_specs=[row((nb, tt, d)), per_b((nb, K_SHORT - 1, W_HALF)), per_b((nb, 1, LANES)),
                  _const_spec(gmix.shape), _const_spec(w_main.shape), _const_spec(w_f.shape),
                  _const_spec(b_f.shape), _const_spec(gq.shape), _const_spec(gk.shape),
                  _const_spec(conv_w.shape), _const_spec(hmat.shape)],
        out_specs=(heads, heads, heads, row((nb, tt, W_HALF)), row((nb, tt, W_HALF)), row((nb, tt, N_HEADS)),
                   row((nb, tt, W_HALF)), per_b((nb, K_SHORT - 1, W_HALF))),
        out_shape=out_shape,
        scratch_shapes=[pltpu.VMEM((nb, tt + SUBLANES, W_HALF), F32), pltpu.VMEM((nb, 1, LANES), F32)],
        compiler_params=pltpu.CompilerParams(dimension_semantics=("arbitrary", "arbitrary"),
                                             vmem_limit_bytes=VMEM_LIMIT),
        name="even_in",
    )(x, hist_b, c0, gmix, w_main, w_f, b_f, gq, gk, conv_w, hmat)


def _cache_prep_kernel(k_ref, v_ref, lf_ref, ka_ref, va_ref, ctot_ref, ccar, *, tt):
    @pl.when(pl.program_id(1) == 0)
    def _():
        ccar[...] = jnp.zeros_like(ccar)

    c2 = _tile_cumsum(lf_ref[0], ccar, 1, tt) * LOG2E
    _augment_heads(k_ref[0], c2, 'k', ka_ref, 1, tt)
    _augment_heads(v_ref[0], None, 'v', va_ref, 1, tt)
    ctot_ref[...] = ccar[...]


def _cache_prep(k, v, lf_pad):
    b, t, _ = k.shape
    tt = min(t, TIME_TILE)
    assert t % tt == 0
    row = lambda w: pl.BlockSpec((1, tt, w), lambda i, j: (i, j, 0))
    heads = pl.BlockSpec((1, N_HEADS, tt, LANES), lambda i, j: (i, 0, j, 0))
    return pl.pallas_call(
        functools.partial(_cache_prep_kernel, tt=tt),
        grid=(b, t // tt),
        in_specs=[row(W_HALF), row(W_HALF), row(LANES)],
        out_specs=(heads, heads, pl.BlockSpec((1, 1, LANES), lambda i, j: (i, 0, 0))),
        out_shape=(jax.ShapeDtypeStruct((b, N_HEADS, t, LANES), BF16),
                   jax.ShapeDtypeStruct((b, N_HEADS, t, LANES), BF16),
                   jax.ShapeDtypeStruct((b, 1, LANES), F32)),
        scratch_shapes=[pltpu.VMEM((1, 1, LANES), F32)],
        compiler_params=pltpu.CompilerParams(dimension_semantics=("arbitrary", "arbitrary"),
                                             vmem_limit_bytes=VMEM_LIMIT),
        name="cache_prep",
    )(k, v, lf_pad)


def _attn_kernel(qa_ref, ka_ref, va_ref, o_ref, *, tq, tk, past):
    i = pl.program_id(2)
    diag_off = past + i * tq
    n_full = diag_off // tk
    nt = (((1,), (1,)), ((), ()))
    qs = [qa_ref[0, hh] for hh in range(2)]
    heads = range(2)

    def scores(hh, off, size):
        return lax.dot_general(qs[hh], ka_ref[0, hh, pl.ds(off, size), :], nt, preferred_element_type=F32)

    def consume(state, hh, s, off, size):
        m, acc = state
        m_new = jnp.maximum(m, jnp.max(s, axis=-1, keepdims=True))
        p = jnp.exp2(s - m_new).astype(BF16)
        pv = jnp.dot(p, va_ref[0, hh, pl.ds(off, size), :], preferred_element_type=F32)
        return m_new, jnp.exp2(m - m_new) * acc + pv

    init = tuple((jnp.full((tq, 1), NEG_BIG, F32), jnp.zeros((tq, LANES), F32)) for _ in heads)

    def body(j, states):
        off = pl.multiple_of(j * tk, tk)
        return tuple(consume(states[hh], hh, scores(hh, off, tk), off, tk) for hh in heads)

    states = lax.fori_loop(0, n_full, body, init)

    doff = pl.multiple_of(diag_off, tq)
    mask = lax.broadcasted_iota(jnp.int32, (tq, tq), 0) >= lax.broadcasted_iota(jnp.int32, (tq, tq), 1)
    outs = []
    for hh in heads:
        _, acc = consume(states[hh], hh, jnp.where(mask, scores(hh, doff, tq), NEG_BIG), doff, tq)
        outs.append(acc / acc[:, D_HEAD:D_HEAD + 1])
    lane = lax.broadcasted_iota(jnp.int32, (1, LANES), 1)
    o_ref[0] = jnp.where(lane < D_HEAD, outs[0], pltpu.roll(outs[1], D_HEAD, axis=1)).astype(BF16)


def _attention(qa, ka, va):
    b, h, t_q, _ = qa.shape
    t_k = ka.shape[2]
    past = t_k - t_q
    tq = min(ATT_BLOCK, t_q)
    tk = tq if past == 0 else math.gcd(past, ATT_CACHE_BLOCK)
    assert t_q % tq == 0 and past % tk == 0 and (tq % tk == 0 or t_q == tq)
    kv_spec = pl.BlockSpec((1, 2, t_k, LANES), lambda i, p, j: (i, p, 0, 0), pipeline_mode=pl.Buffered(1))
    return pl.pallas_call(
        functools.partial(_attn_kernel, tq=tq, tk=tk, past=past),
        grid=(b, h // 2, t_q // tq),
        in_specs=[pl.BlockSpec((1, 2, tq, LANES), lambda i, p, j: (i, p, j, 0)), kv_spec, kv_spec],
        out_specs=pl.BlockSpec((1, tq, LANES), lambda i, p, j: (i, j, p)),
        out_shape=jax.ShapeDtypeStruct((b, t_q, h * D_HEAD), BF16),
        compiler_params=pltpu.CompilerParams(dimension_semantics=("arbitrary", "arbitrary", "arbitrary"),
                                             vmem_limit_bytes=VMEM_LIMIT),
        name="fox_attention",
    )(qa, ka, va)


def _gelu_tanh(x):
    return 0.5 * x * (1.0 + jnp.tanh(math.sqrt(2.0 / math.pi) * (x + 0.044715 * (x * x * x))))


def _long_conv(dbuf, phase, cw_ref, cd_ref, nb, tt, first):
    rows_kept = tt + first + K_LONG - 1 - SUBLANES
    for r in range(1, SUBLANES):
        phase[r - 1, :, :rows_kept, :] = dbuf[:, r:r + rows_kept, :]
    rb = CONV_ACC_ROWS // nb
    for cb in range(W_HALF // LANES):
        cols = slice(cb * LANES, (cb + 1) * LANES)
        w_cols = cw_ref[:, cols]
        for r0 in range(0, tt, rb):
            acc = None
            for k in range(K_LONG):
                r, a = (first + k) % SUBLANES, (first + k) // SUBLANES * SUBLANES
                src = dbuf if r == 0 else phase.at[r - 1]
                term = w_cols[k:k + 1] * src[:, a + r0:a + r0 + rb, cols]
                acc = term if acc is None else acc + term
            cd_ref[:, r0:r0 + rb, cols] = acc


def _odd_in_kernel(x_ref, hist_ref, gmix_ref, w_ref, gvc_ref, ws_ref, bs_ref, cw_ref, gd_ref,
                   oc_ref, od_ref, hout_ref, *rest, nb, tt, cs, emit_vc):
    vc_ref = rest[0] if emit_vc else None
    dbuf, phase, cdbuf = rest[-3:]
    tm = nb * tt
    pad = 4 * SUBLANES
    nh = K_LONG - 1

    @pl.when(pl.program_id(1) == 0)
    def _():
        dbuf[:, pad - nh:pad, :] = hist_ref[...]

    x = x_ref[...].reshape(tm, x_ref.shape[-1])
    xn = _rms(x, gmix_ref[...]).astype(BF16)

    def proj(i):
        return jnp.dot(xn, w_ref[:, i * W_HALF:(i + 1) * W_HALF], preferred_element_type=F32)

    u = _gelu_tanh(proj(0))
    vc = _rms(_gelu_tanh(proj(1)), gvc_ref[...])
    if emit_vc:
        vc_ref[...] = vc.reshape(nb, tt, W_HALF)
    vcb = vc.astype(BF16)

    row = lax.broadcasted_iota(jnp.int32, (cs, cs), 0)
    col = lax.broadcasted_iota(jnp.int32, (cs, cs), 1)
    per_b = tt // cs
    for g in range(N_GROUPS):
        wsg = jnp.where(row >= col, ws_ref[g], 0.0).astype(BF16)
        for ci in range(tm // cs):
            r0 = ci * cs
            gate = jnp.dot(wsg, vcb[r0:r0 + cs, g * LANES:(g + 1) * LANES], preferred_element_type=F32) + bs_ref[g]
            oc = u[r0:r0 + cs, g * LANES:(g + 1) * LANES] * gate
            t0 = (ci % per_b) * cs
            oc_ref[ci // per_b, t0:t0 + cs, g * LANES:(g + 1) * LANES] = oc.astype(BF16)

    glu = proj(2) * jax.nn.sigmoid(proj(3))
    dbuf[:, pad:pad + tt, :] = glu.reshape(nb, tt, W_HALF)
    _long_conv(dbuf, phase, cw_ref, cdbuf, nb, tt, pad - nh)
    y = _rms(cdbuf[...].reshape(tm, W_HALF), gd_ref[...])
    od_ref[...] = (y * jax.nn.sigmoid(y)).astype(BF16).reshape(nb, tt, W_HALF)
    tail = dbuf[:, pad + tt - nh:pad + tt, :]
    dbuf[:, pad - nh:pad, :] = tail
    hout_ref[...] = tail


def _odd_in(x, hist_d, gmix, w_in, gvc, ws, bs, conv_w, gd, emit_vc):
    b, t, d = x.shape
    assert t >= K_LONG - 1
    nb, tt = _tiles(b, t)
    cs = min(GMLP_CHUNK, tt)
    assert tt % cs == 0 and CONV_ACC_ROWS % nb == 0 and tt % (CONV_ACC_ROWS // nb) == 0
    grid = (b // nb, t // tt)
    row = lambda shape: pl.BlockSpec(shape, lambda i, j: (i, j, 0))
    per_b = lambda shape: pl.BlockSpec(shape, lambda i, j: (i, 0, 0))
    ws = ws[:, :cs, :cs]
    bs = bs[:, :cs, :]
    out_specs = [row((nb, tt, W_HALF)), row((nb, tt, W_HALF)), per_b((nb, K_LONG - 1, W_HALF))]
    out_shape = [jax.ShapeDtypeStruct((b, t, W_HALF), BF16), jax.ShapeDtypeStruct((b, t, W_HALF), BF16),
                 jax.ShapeDtypeStruct((b, K_LONG - 1, W_HALF), F32)]
    if emit_vc:
        out_specs.append(row((nb, tt, W_HALF)))
        out_shape.append(jax.ShapeDtypeStruct((b, t, W_HALF), F32))
    hist_rows = 4 * SUBLANES
    return pl.pallas_call(
        functools.partial(_odd_in_kernel, nb=nb, tt=tt, cs=cs, emit_vc=emit_vc),
        grid=grid,
        in_specs=[row((nb, tt, d)), per_b((nb, K_LONG - 1, W_HALF)),
                  _const_spec(gmix.shape), _const_spec(w_in.shape), _const_spec(gvc.shape),
                  _const_spec(ws.shape), _const_spec(bs.shape), _const_spec(conv_w.shape), _const_spec(gd.shape)],
        out_specs=tuple(out_specs),
        out_shape=tuple(out_shape),
        scratch_shapes=[pltpu.VMEM((nb, tt + hist_rows, W_HALF), F32),
                        pltpu.VMEM((SUBLANES - 1, nb, tt + hist_rows - SUBLANES, W_HALF), F32),
                        pltpu.VMEM((nb, tt, W_HALF), F32)],
        compiler_params=pltpu.CompilerParams(dimension_semantics=("arbitrary", "arbitrary"),
                                             vmem_limit_bytes=VMEM_LIMIT),
        name="odd_in",
    )(x, hist_d, gmix, w_in, gvc, ws, bs, conv_w, gd)


def _mix_ffn_kernel(x_ref, oa_ref, ob_ref, hist_ref, wo_ref, g_ref, wu_ref, cw_ref, wd_ref,
                    y_ref, hout_ref, hbuf, abuf, *, nb, tt, n_chunks):
    tm = nb * tt
    pad = SUBLANES
    d_ff = wd_ref.shape[0]
    ch = d_ff // n_chunks

    @pl.when(pl.program_id(1) == 0)
    def _():
        hbuf[:, pad - (K_SHORT - 1):pad, :] = hist_ref[...]

    d = x_ref.shape[-1]
    oa = oa_ref[...].reshape(tm, W_HALF)
    ob = ob_ref[...].reshape(tm, W_HALF)
    x1 = (x_ref[...].reshape(tm, d)
          + jnp.dot(oa, wo_ref[:W_HALF, :], preferred_element_type=F32)
          + jnp.dot(ob, wo_ref[W_HALF:, :], preferred_element_type=F32))
    xn = _rms(x1, g_ref[...]).astype(BF16)

    def conv_up(lo):
        h = jnp.dot(xn, wu_ref[:, lo:lo + ch], preferred_element_type=F32)
        hbuf[:, pad:pad + tt, lo:lo + ch] = h.reshape(nb, tt, ch)
        cw = cw_ref[:, lo:lo + ch]
        return (cw[2:3] * hbuf[:, pad:pad + tt, lo:lo + ch]
                + cw[1:2] * hbuf[:, pad - 1:pad - 1 + tt, lo:lo + ch]
                + cw[0:1] * hbuf[:, pad - 2:pad - 2 + tt, lo:lo + ch]).reshape(tm, ch)

    for c in range(n_chunks):
        gate = conv_up(c * ch)
        val = conv_up(d_ff + c * ch)
        abuf[:, c * ch:(c + 1) * ch] = (gate * jax.nn.sigmoid(gate) * val).astype(BF16)
    y_ref[...] = (x1 + jnp.dot(abuf[...], wd_ref[...], preferred_element_type=F32)).reshape(nb, tt, d)
    tail = hbuf[:, pad + tt - (K_SHORT - 1):pad + tt, :]
    hbuf[:, pad - (K_SHORT - 1):pad, :] = tail
    hout_ref[...] = tail


def _mix_ffn(x, oa, ob, hist_ffn, w_out, g_ffn, w_up, conv_w, w_down, n_chunks=11):
    b, t, d = x.shape
    assert t >= K_SHORT - 1
    nb, tt = _tiles(b, t)
    grid = (b // nb, t // tt)
    two_ff = w_up.shape[1]
    row = lambda shape: pl.BlockSpec(shape, lambda i, j: (i, j, 0))
    per_b = lambda shape: pl.BlockSpec(shape, lambda i, j: (i, 0, 0))
    return pl.pallas_call(
        functools.partial(_mix_ffn_kernel, nb=nb, tt=tt, n_chunks=n_chunks),
        grid=grid,
        in_specs=[row((nb, tt, d)), row((nb, tt, W_HALF)), row((nb, tt, W_HALF)), per_b((nb, K_SHORT - 1, two_ff)),
                  _const_spec(w_out.shape), _const_spec(g_ffn.shape), _const_spec(w_up.shape),
                  _const_spec(conv_w.shape), _const_spec(w_down.shape)],
        out_specs=(row((nb, tt, d)), per_b((nb, K_SHORT - 1, two_ff))),
        out_shape=(jax.ShapeDtypeStruct((b, t, d), F32),
                   jax.ShapeDtypeStruct((b, K_SHORT - 1, two_ff), F32)),
        scratch_shapes=[pltpu.VMEM((nb, tt + SUBLANES, two_ff), F32), pltpu.VMEM((nb * tt, two_ff // 2), BF16)],
        compiler_params=pltpu.CompilerParams(dimension_semantics=("arbitrary", "arbitrary"),
                                             vmem_limit_bytes=VMEM_LIMIT),
        name="mix_ffn",
    )(x, oa, ob, hist_ffn, w_out, g_ffn, w_up, conv_w, w_down)


def _prep_params(g_mix, w_in_even, b_f, g_q, g_k, conv_b, w_out_even, w_in_odd, g_vc, w_s, b_s,
                 conv_d, g_d, w_out_odd, g_ffn, w_up, conv_ffn, w_down):
    depth = g_mix.shape[0]
    off_f = 3 * W_HALF
    head_id = jnp.arange(W_HALF) // D_HEAD
    hmat = jnp.where(head_id[:, None] == head_id[None, :], 1.0 / D_HEAD, 0.0).astype(BF16)
    layers = []
    for l in range(depth):
        i = l // 2
        p = dict(g_mix=g_mix[l][None], g_ffn=g_ffn[l][None], w_up=w_up[l].astype(BF16),
                 conv_ffn=conv_ffn[l], w_down=w_down[l].astype(BF16))
        if l % 2 == 0:
            w = w_in_even[i]
            p.update(
                w_main=jnp.concatenate([w[:, :off_f], w[:, off_f + N_HEADS:]], axis=1).astype(BF16),
                w_f=jnp.pad(w[:, off_f:off_f + N_HEADS], ((0, 0), (0, LANES - N_HEADS))).astype(BF16),
                b_f=jnp.pad(b_f[i], (0, LANES - N_HEADS))[None],
                g_q=jnp.tile(g_q[i], N_HEADS)[None], g_k=jnp.tile(g_k[i], N_HEADS)[None],
                conv_w=conv_b[i], w_out=w_out_even[i].astype(BF16), hmat=hmat)
        else:
            p.update(
                w_in=w_in_odd[i].astype(BF16), g_vc=g_vc[i][None], w_s=w_s[i],
                b_s=jnp.broadcast_to(b_s[i][:, :, None], b_s[i].shape + (LANES,)),
                conv_w=conv_d[i], g_d=g_d[i][None], w_out=w_out_odd[i].astype(BF16))
        layers.append(p)
    return layers


def _trunk(x, layers, hist_b, hist_d, hist_ffn, att_cache):
    b = x.shape[0]
    ks, vs, lfs, hbs, vcs, hds, hfs = [], [], [], [], [], [], []
    for l, p in enumerate(layers):
        i = l // 2
        if l % 2 == 0:
            if att_cache is None:
                c0 = jnp.zeros((b, 1, LANES), F32)
            else:
                ck, cv, clf = att_cache[0][i], att_cache[1][i], att_cache[2][i]
                past = ck.shape[1]
                ka_c, va_c, c0 = _cache_prep(ck.reshape(b, past, W_HALF), cv.reshape(b, past, W_HALF),
                                             jnp.pad(clf, ((0, 0), (0, 0), (0, LANES - N_HEADS))))
            qa, ka, va, k, v, lf, ob, hb = _even_in(x, hist_b[i], c0, p['g_mix'], p['w_main'], p['w_f'], p['b_f'],
                                                    p['g_q'], p['g_k'], p['conv_w'], p['hmat'])
            if att_cache is not None:
                ka = jnp.concatenate([ka_c, ka], axis=2)
                va = jnp.concatenate([va_c, va], axis=2)
            oa = _attention(qa, ka, va)
            t = x.shape[1]
            ks.append(k.reshape(b, t, N_HEADS, D_HEAD))
            vs.append(v.reshape(b, t, N_HEADS, D_HEAD))
            lfs.append(lf)
            hbs.append(hb)
        else:
            oa, ob, hd, *vc = _odd_in(x, hist_d[i], p['g_mix'], p['w_in'], p['g_vc'], p['w_s'], p['b_s'],
                                      p['conv_w'], p['g_d'], emit_vc=att_cache is not None)
            vcs.extend(vc)
            hds.append(hd)
        x, hf = _mix_ffn(x, oa, ob, hist_ffn[l], p['w_out'], p['g_ffn'], p['w_up'], p['conv_ffn'], p['w_down'])
        hfs.append(hf)
    return (x, jnp.stack(ks), jnp.stack(vs), jnp.stack(lfs), jnp.stack(hbs),
            jnp.stack(vcs) if vcs else None, jnp.stack(hds), jnp.stack(hfs))


def kernel(x_prompt, x_sample, cache_k, cache_v, cache_logf, state_conv_b, state_conv_d, state_conv_ffn,
           g_mix, w_in_even, b_f, g_q, g_k, conv_b, w_out_even, w_in_odd, g_vc, w_s, b_s,
           conv_d, g_d, w_out_odd, g_ffn, w_up, conv_ffn, w_down):
    layers = _prep_params(g_mix, w_in_even, b_f, g_q, g_k, conv_b, w_out_even, w_in_odd, g_vc, w_s, b_s,
                          conv_d, g_d, w_out_odd, g_ffn, w_up, conv_ffn, w_down)
    b = x_prompt.shape[0]
    depth = g_mix.shape[0]
    n_even, n_odd = (depth + 1) // 2, depth // 2
    zb = jnp.zeros((n_even, b, K_SHORT - 1, W_HALF), F32)
    zd = jnp.zeros((n_odd, b, K_LONG - 1, W_HALF), F32)
    zf = jnp.zeros((depth, b, K_SHORT - 1, w_up.shape[-1]), F32)
    (y_prompt, p_k, p_v, p_logf, p_conv_b, _, p_conv_d, p_conv_ffn) = _trunk(
        x_prompt, layers, zb, zd, zf, None)
    (y_sample, s_k, s_v, s_logf, s_conv_b, s_vc, s_conv_d, s_conv_ffn) = _trunk(
        x_sample, layers, state_conv_b, state_conv_d, state_conv_ffn, (cache_k, cache_v, cache_logf))
    return (y_prompt, y_sample, p_k, p_v, p_logf, p_conv_b, p_conv_d, p_conv_ffn,
            s_k, s_v, s_logf, s_conv_b, s_vc, s_conv_d, s_conv_ffn)
```

```python
import functools
import math

import jax
import jax.numpy as jnp
from jax import lax
from jax.experimental import pallas as pl
from jax.experimental.pallas import tpu as pltpu

F32 = jnp.float32
BF16 = jnp.bfloat16

EPS = 1e-6
LOG2E = math.log2(math.e)
LANES = 128
SUBLANES = 8
N_HEADS = 8
D_HEAD = 64
W_HALF = 512
K_SHORT = 3
K_LONG = 31
GMLP_CHUNK = 128
N_GROUPS = 4
TIME_TILE = 512
ATT_BLOCK = 1024
ATT_CACHE_BLOCK = 4096
NEG_BIG = -1e30
EXP2_UNDERFLOW = 150.0
CONV_ACC_ROWS = 128
VMEM_LIMIT = 56 * 1024 * 1024


def _rms(x, g):
    return x * lax.rsqrt(jnp.mean(x * x, axis=-1, keepdims=True) + EPS) * g


def _split3(x):
    hi = x.astype(BF16).astype(F32)
    r = x - hi
    mid = r.astype(BF16).astype(F32)
    lo = r - mid
    return hi, mid, lo


def _tile_cumsum(lf, carry_ref, nb, tt):
    row = lax.broadcasted_iota(jnp.int32, (tt, tt), 0)
    col = lax.broadcasted_iota(jnp.int32, (tt, tt), 1)
    tri = jnp.where(row >= col, 1.0, 0.0).astype(BF16)
    hi, mid, lo = _split3(lf)
    cat = jnp.concatenate([hi, mid, lo], axis=1).astype(BF16)
    outs = []
    for b in range(nb):
        r = jnp.dot(tri, cat[b * tt:(b + 1) * tt], preferred_element_type=F32)
        c_b = r[:, :LANES] + r[:, LANES:2 * LANES] + r[:, 2 * LANES:] + carry_ref[b]
        carry_ref[b] = c_b[tt - 1:tt, :]
        outs.append(c_b)
    return outs[0] if nb == 1 else jnp.concatenate(outs, axis=0)


def _lane_consts():
    lane = lax.broadcasted_iota(jnp.int32, (1, LANES), 1)
    e = lambda i: jnp.where(lane == i, 1.0, 0.0).astype(F32)
    ones_a = e(D_HEAD) + e(D_HEAD + 1) + e(D_HEAD + 2)
    ones_b = e(D_HEAD + 3) + e(D_HEAD + 4) + e(D_HEAD + 5)
    return lane < D_HEAD, e, ones_a, ones_b


def _augment_heads(a, c2, kind, out_ref, nb, tt):
    head_lanes, e, ones_a, ones_b = _lane_consts()
    tm = nb * tt
    for h in range(N_HEADS):
        pair = a[:, LANES * (h // 2):LANES * (h // 2 + 1)]
        if h % 2:
            pair = pltpu.roll(pair, D_HEAD, axis=1)
        if kind == 'v':
            ext = e(D_HEAD)
        else:
            cb = jnp.broadcast_to(c2[:, h:h + 1], (tm, LANES))
            hi, mid, lo = _split3(cb)
            if kind == 'q':
                ext = hi * e(D_HEAD) + mid * e(D_HEAD + 1) + lo * e(D_HEAD + 2) + ones_b
            else:
                ext = ones_a - (hi * e(D_HEAD + 3) + mid * e(D_HEAD + 4) + lo * e(D_HEAD + 5))
        out_ref[:, h] = jnp.where(head_lanes, pair, ext).astype(BF16).reshape(nb, tt, LANES)


def _log_sigmoid(z):
    return -(jnp.maximum(-z, 0.0) + jnp.log1p(jnp.exp(-jnp.abs(z))))


def _masked_logf(z):
    lane = lax.broadcasted_iota(jnp.int32, (1, LANES), 1)
    return jnp.where(lane < N_HEADS, _log_sigmoid(z), 0.0)


def _even_in_kernel(x_ref, hist_ref, c0_ref, gmix_ref, w_ref, wf_ref, bf_ref, gq_ref, gk_ref, cw_ref, hm_ref,
                    qa_ref, ka_ref, va_ref, k_ref, v_ref, lf_ref, ob_ref, hout_ref, edge_ref,
                    cbuf, ccar, *, nb, tt, kv_time_minor):
    tm = nb * tt
    pad = SUBLANES

    @pl.when(pl.program_id(1) == 0)
    def _():
        cbuf[:, pad - (K_SHORT - 1):pad, :] = hist_ref[...]
        ccar[...] = c0_ref[...]

    x = x_ref[...].reshape(tm, x_ref.shape[-1])
    xn = _rms(x, gmix_ref[...]).astype(BF16)

    def proj(i):
        return jnp.dot(xn, w_ref[:, i * W_HALF:(i + 1) * W_HALF], preferred_element_type=F32)

    hm = hm_ref[...]

    def headnorm(a, g):
        ms = jnp.dot((a * a).astype(BF16), hm, preferred_element_type=F32)
        return a * lax.rsqrt(ms + EPS) * g

    qn = headnorm(proj(0), gq_ref[...])
    kn = headnorm(proj(1), gk_ref[...])
    v = proj(2)
    if kv_time_minor:
        for b in range(nb):
            k_ref[b] = kn[b * tt:(b + 1) * tt, :].T.reshape(N_HEADS, D_HEAD, tt)
            v_ref[b] = v[b * tt:(b + 1) * tt, :].T.reshape(N_HEADS, D_HEAD, tt)
    else:
        k_ref[...] = kn.reshape(nb, tt, W_HALF)
        v_ref[...] = v.reshape(nb, tt, W_HALF)

    z = jnp.dot(xn, wf_ref[...], preferred_element_type=F32) + bf_ref[...]
    lf = _masked_logf(z)
    lf_ref[...] = lf[:, :N_HEADS].reshape(nb, tt, N_HEADS)
    c2 = _tile_cumsum(lf, ccar, nb, tt) * LOG2E
    for b in range(nb):
        edge_ref[b, 0, 0:1, :] = c2[b * tt:b * tt + 1, :]
        edge_ref[b, 0, 1:2, :] = c2[(b + 1) * tt - 1:(b + 1) * tt, :]
        edge_ref[b, 0, 2:, :] = jnp.zeros((SUBLANES - 2, LANES), F32)

    _augment_heads(qn * (LOG2E * D_HEAD ** -0.5), c2, 'q', qa_ref, nb, tt)
    _augment_heads(kn, c2, 'k', ka_ref, nb, tt)
    _augment_heads(v, None, 'v', va_ref, nb, tt)

    bg = proj(3)
    cgx = proj(4) * proj(5)
    cbuf[:, pad:pad + tt, :] = cgx.reshape(nb, tt, W_HALF)
    cw = cw_ref[...]
    cx = (cw[2:3] * cbuf[:, pad:pad + tt, :] + cw[1:2] * cbuf[:, pad - 1:pad - 1 + tt, :]
          + cw[0:1] * cbuf[:, pad - 2:pad - 2 + tt, :])
    ob_ref[...] = (bg.reshape(nb, tt, W_HALF) * cx).astype(BF16)
    tail = cbuf[:, pad + tt - (K_SHORT - 1):pad + tt, :]
    cbuf[:, pad - (K_SHORT - 1):pad, :] = tail
    hout_ref[...] = tail


def _const_spec(shape, single=True):
    idx = lambda *_: (0,) * len(shape)
    if single:
        return pl.BlockSpec(shape, idx, pipeline_mode=pl.Buffered(1))
    return pl.BlockSpec(shape, idx)


def _tiles(b, t):
    if t >= TIME_TILE:
        assert t % TIME_TILE == 0
        return 1, TIME_TILE
    assert t % (2 * SUBLANES) == 0
    return b, t


def _even_in(x, hist_b, c0, gmix, w_main, w_f, b_f, gq, gk, conv_w, hmat):
    b, t, d = x.shape
    assert t >= K_SHORT - 1
    nb, tt = _tiles(b, t)
    grid = (b // nb, t // tt)
    row = lambda shape: pl.BlockSpec(shape, lambda i, j: (i, j, 0))
    per_b = lambda shape: pl.BlockSpec(shape, lambda i, j: (i, 0, 0))
    heads = pl.BlockSpec((nb, N_HEADS, tt, LANES), lambda i, j: (i, 0, j, 0))
    kv_time_minor = tt % LANES == 0
    if kv_time_minor:
        kv_shape = jax.ShapeDtypeStruct((b, N_HEADS, D_HEAD, t), F32)
        kv_spec = pl.BlockSpec((nb, N_HEADS, D_HEAD, tt), lambda i, j: (i, 0, 0, j))
    else:
        kv_shape = jax.ShapeDtypeStruct((b, t, W_HALF), F32)
        kv_spec = row((nb, tt, W_HALF))
    out_shape = (
        jax.ShapeDtypeStruct((b, N_HEADS, t, LANES), BF16),
        jax.ShapeDtypeStruct((b, N_HEADS, t, LANES), BF16),
        jax.ShapeDtypeStruct((b, N_HEADS, t, LANES), BF16),
        kv_shape,
        kv_shape,
        jax.ShapeDtypeStruct((b, t, N_HEADS), F32),
        jax.ShapeDtypeStruct((b, t, W_HALF), BF16),
        jax.ShapeDtypeStruct((b, K_SHORT - 1, W_HALF), F32),
        jax.ShapeDtypeStruct((b, t // tt, SUBLANES, LANES), F32),
    )
    return pl.pallas_call(
        functools.partial(_even_in_kernel, nb=nb, tt=tt, kv_time_minor=kv_time_minor),
        grid=grid,
        in_specs=[row((nb, tt, d)), per_b((nb, K_SHORT - 1, W_HALF)), per_b((nb, 1, LANES)),
                  _const_spec(gmix.shape), _const_spec(w_main.shape), _const_spec(w_f.shape),
                  _const_spec(b_f.shape), _const_spec(gq.shape), _const_spec(gk.shape),
                  _const_spec(conv_w.shape), _const_spec(hmat.shape)],
        out_specs=(heads, heads, heads, kv_spec, kv_spec, row((nb, tt, N_HEADS)),
                   row((nb, tt, W_HALF)), per_b((nb, K_SHORT - 1, W_HALF)),
                   pl.BlockSpec((nb, 1, SUBLANES, LANES), lambda i, j: (i, j, 0, 0))),
        out_shape=out_shape,
        scratch_shapes=[pltpu.VMEM((nb, tt + SUBLANES, W_HALF), F32), pltpu.VMEM((nb, 1, LANES), F32)],
        compiler_params=pltpu.CompilerParams(dimension_semantics=("arbitrary", "arbitrary"),
                                             vmem_limit_bytes=VMEM_LIMIT),
        name="even_in",
    )(x, hist_b, c0, gmix, w_main, w_f, b_f, gq, gk, conv_w, hmat)


def _cache_prep_kernel(k_ref, v_ref, lf_ref, ka_ref, va_ref, ctot_ref, ccar, *, tt):
    @pl.when(pl.program_id(1) == 0)
    def _():
        ccar[...] = jnp.zeros_like(ccar)

    c2 = _tile_cumsum(lf_ref[0], ccar, 1, tt) * LOG2E
    _augment_heads(k_ref[0], c2, 'k', ka_ref, 1, tt)
    _augment_heads(v_ref[0], None, 'v', va_ref, 1, tt)
    ctot_ref[...] = ccar[...]


def _cache_prep(k, v, lf_pad, layer):
    _, b, t, _ = k.shape
    tt = min(t, TIME_TILE)
    assert t % tt == 0
    row = lambda w: pl.BlockSpec((None, 1, tt, w), lambda i, j: (layer, i, j, 0))
    heads = pl.BlockSpec((1, N_HEADS, tt, LANES), lambda i, j: (i, 0, j, 0))
    return pl.pallas_call(
        functools.partial(_cache_prep_kernel, tt=tt),
        grid=(b, t // tt),
        in_specs=[row(W_HALF), row(W_HALF), row(LANES)],
        out_specs=(heads, heads, pl.BlockSpec((1, 1, LANES), lambda i, j: (i, 0, 0))),
        out_shape=(jax.ShapeDtypeStruct((b, N_HEADS, t, LANES), BF16),
                   jax.ShapeDtypeStruct((b, N_HEADS, t, LANES), BF16),
                   jax.ShapeDtypeStruct((b, 1, LANES), F32)),
        scratch_shapes=[pltpu.VMEM((1, 1, LANES), F32)],
        compiler_params=pltpu.CompilerParams(dimension_semantics=("arbitrary", "arbitrary"),
                                             vmem_limit_bytes=VMEM_LIMIT),
        name="cache_prep",
    )(k, v, lf_pad)


def _attn_kernel(thr_ref, klast_ref, qa_ref, ka_ref, va_ref, o_ref, *, tq, tk, past, skip):
    i = pl.program_id(2)
    diag_off = past + i * tq
    n_full = diag_off // tk
    nt = (((1,), (1,)), ((), ()))
    qs = [qa_ref[0, hh] for hh in range(2)]
    heads = range(2)

    def scores(hh, off, size):
        return lax.dot_general(qs[hh], ka_ref[0, hh, pl.ds(off, size), :], nt, preferred_element_type=F32)

    def consume(state, hh, s, off, size):
        m, acc = state
        m_new = jnp.maximum(m, jnp.max(s, axis=-1, keepdims=True))
        p = jnp.exp2(s - m_new).astype(BF16)
        pv = jnp.dot(p, va_ref[0, hh, pl.ds(off, size), :], preferred_element_type=F32)
        return m_new, jnp.exp2(m - m_new) * acc + pv

    init = tuple((jnp.full((tq, 1), NEG_BIG, F32), jnp.zeros((tq, LANES), F32)) for _ in heads)

    def body(j, states):
        off = pl.multiple_of(j * tk, tk)
        return tuple(consume(states[hh], hh, scores(hh, off, tk), off, tk) for hh in heads)

    first = 0
    if skip:
        def skippable(hh):
            r = (pl.program_id(0) * pl.num_programs(1) + pl.program_id(1)) * 2 + hh
            thr = thr_ref[r, i]
            return lax.fori_loop(0, n_full, lambda j, n: n + (klast_ref[r, j] >= thr).astype(jnp.int32), 0)

        first = jnp.minimum(skippable(0), skippable(1))
    states = lax.fori_loop(first, n_full, body, init)

    doff = pl.multiple_of(diag_off, tq)
    mask = lax.broadcasted_iota(jnp.int32, (tq, tq), 0) >= lax.broadcasted_iota(jnp.int32, (tq, tq), 1)
    outs = []
    for hh in heads:
        _, acc = consume(states[hh], hh, jnp.where(mask, scores(hh, doff, tq), NEG_BIG), doff, tq)
        outs.append(acc / acc[:, D_HEAD:D_HEAD + 1])
    lane = lax.broadcasted_iota(jnp.int32, (1, LANES), 1)
    o_ref[0] = jnp.where(lane < D_HEAD, outs[0], pltpu.roll(outs[1], D_HEAD, axis=1)).astype(BF16)


def _skip_tables(edges, tile, block, g_q, g_k):
    b = edges.shape[0]
    per = block // tile
    qk_bound = 1.02 * D_HEAD ** 0.5 * LOG2E * jnp.max(jnp.abs(g_q)) * jnp.max(jnp.abs(g_k))
    margin = 2.0 * qk_bound + (EXP2_UNDERFLOW + 1.0)
    q_first = edges[:, 0::per, 0, :N_HEADS]
    k_last = edges[:, per - 1::per, 1, :N_HEADS]
    to_rows = lambda a: jnp.transpose(a, (0, 2, 1)).reshape(b * N_HEADS, -1)
    return to_rows(q_first + margin), to_rows(k_last)


def _attention(qa, ka, va, skip_tables=None):
    b, h, t_q, _ = qa.shape
    t_k = ka.shape[2]
    past = t_k - t_q
    tq = min(ATT_BLOCK, t_q)
    tk = tq if past == 0 else math.gcd(past, ATT_CACHE_BLOCK)
    assert t_q % tq == 0 and past % tk == 0 and (tq % tk == 0 or t_q == tq)
    skip = skip_tables is not None
    if skip:
        assert past == 0 and tq == tk and h == N_HEADS
        thr, k_last = skip_tables
    else:
        thr = k_last = jnp.zeros((1, 1), F32)
    kv_bytes = 2 * t_k * LANES * 2
    kv_mode = dict(pipeline_mode=pl.Buffered(1)) if 4 * kv_bytes > VMEM_LIMIT // 4 else {}
    kv_spec = pl.BlockSpec((1, 2, t_k, LANES), lambda i, p, j: (i, p, 0, 0), **kv_mode)
    smem = pl.BlockSpec(memory_space=pltpu.SMEM)
    return pl.pallas_call(
        functools.partial(_attn_kernel, tq=tq, tk=tk, past=past, skip=skip),
        grid=(b, h // 2, t_q // tq),
        in_specs=[smem, smem, pl.BlockSpec((1, 2, tq, LANES), lambda i, p, j: (i, p, j, 0)), kv_spec, kv_spec],
        out_specs=pl.BlockSpec((1, tq, LANES), lambda i, p, j: (i, j, p)),
        out_shape=jax.ShapeDtypeStruct((b, t_q, h * D_HEAD), BF16),
        compiler_params=pltpu.CompilerParams(dimension_semantics=("arbitrary", "arbitrary", "arbitrary"),
                                             vmem_limit_bytes=VMEM_LIMIT),
        name="fox_attention",
    )(thr, k_last, qa, ka, va)


def _gelu_tanh(x):
    return 0.5 * x * (1.0 + jnp.tanh(math.sqrt(2.0 / math.pi) * (x + 0.044715 * (x * x * x))))


def _long_conv(dbuf, phase, cw_ref, cd_ref, nb, tt, first):
    rows_kept = tt + first + K_LONG - 1 - SUBLANES
    for r in range(1, SUBLANES):
        phase[r - 1, :, :rows_kept, :] = dbuf[:, r:r + rows_kept, :]
    rb = CONV_ACC_ROWS // nb
    for cb in range(W_HALF // LANES):
        cols = slice(cb * LANES, (cb + 1) * LANES)
        w_cols = cw_ref[:, cols]
        for r0 in range(0, tt, rb):
            acc = None
            for k in range(K_LONG):
                r, a = (first + k) % SUBLANES, (first + k) // SUBLANES * SUBLANES
                src = dbuf if r == 0 else phase.at[r - 1]
                term = w_cols[k:k + 1] * src[:, a + r0:a + r0 + rb, cols]
                acc = term if acc is None else acc + term
            cd_ref[:, r0:r0 + rb, cols] = acc


def _odd_in_kernel(x_ref, hist_ref, gmix_ref, w_ref, gvc_ref, ws_ref, bs_ref, cw_ref, gd_ref,
                   oc_ref, od_ref, hout_ref, *rest, nb, tt, cs, emit_vc):
    vc_ref = rest[0] if emit_vc else None
    dbuf, phase, cdbuf = rest[-3:]
    tm = nb * tt
    pad = 4 * SUBLANES
    nh = K_LONG - 1

    @pl.when(pl.program_id(1) == 0)
    def _():
        dbuf[:, pad - nh:pad, :] = hist_ref[...]

    x = x_ref[...].reshape(tm, x_ref.shape[-1])
    xn = _rms(x, gmix_ref[...]).astype(BF16)

    def proj(i):
        return jnp.dot(xn, w_ref[:, i * W_HALF:(i + 1) * W_HALF], preferred_element_type=F32)

    u = _gelu_tanh(proj(0))
    vc = _rms(_gelu_tanh(proj(1)), gvc_ref[...])
    if emit_vc:
        vc_ref[...] = vc.reshape(nb, tt, W_HALF)
    vcb = vc.astype(BF16)

    row = lax.broadcasted_iota(jnp.int32, (cs, cs), 0)
    col = lax.broadcasted_iota(jnp.int32, (cs, cs), 1)
    per_b = tt // cs
    for g in range(N_GROUPS):
        wsg = jnp.where(row >= col, ws_ref[g], 0.0).astype(BF16)
        for ci in range(tm // cs):
            r0 = ci * cs
            gate = jnp.dot(wsg, vcb[r0:r0 + cs, g * LANES:(g + 1) * LANES], preferred_element_type=F32) + bs_ref[g]
            oc = u[r0:r0 + cs, g * LANES:(g + 1) * LANES] * gate
            t0 = (ci % per_b) * cs
            oc_ref[ci // per_b, t0:t0 + cs, g * LANES:(g + 1) * LANES] = oc.astype(BF16)

    glu = proj(2) * jax.nn.sigmoid(proj(3))
    dbuf[:, pad:pad + tt, :] = glu.reshape(nb, tt, W_HALF)
    _long_conv(dbuf, phase, cw_ref, cdbuf, nb, tt, pad - nh)
    y = _rms(cdbuf[...].reshape(tm, W_HALF), gd_ref[...])
    od_ref[...] = (y * jax.nn.sigmoid(y)).astype(BF16).reshape(nb, tt, W_HALF)
    tail = dbuf[:, pad + tt - nh:pad + tt, :]
    dbuf[:, pad - nh:pad, :] = tail
    hout_ref[...] = tail


def _odd_in(x, hist_d, gmix, w_in, gvc, ws, bs, conv_w, gd, emit_vc):
    b, t, d = x.shape
    assert t >= K_LONG - 1
    nb, tt = _tiles(b, t)
    cs = min(GMLP_CHUNK, tt)
    assert tt % cs == 0 and CONV_ACC_ROWS % nb == 0 and tt % (CONV_ACC_ROWS // nb) == 0
    grid = (b // nb, t // tt)
    row = lambda shape: pl.BlockSpec(shape, lambda i, j: (i, j, 0))
    per_b = lambda shape: pl.BlockSpec(shape, lambda i, j: (i, 0, 0))
    ws = ws[:, :cs, :cs]
    bs = bs[:, :cs, :]
    out_specs = [row((nb, tt, W_HALF)), row((nb, tt, W_HALF)), per_b((nb, K_LONG - 1, W_HALF))]
    out_shape = [jax.ShapeDtypeStruct((b, t, W_HALF), BF16), jax.ShapeDtypeStruct((b, t, W_HALF), BF16),
                 jax.ShapeDtypeStruct((b, K_LONG - 1, W_HALF), F32)]
    if emit_vc:
        out_specs.append(row((nb, tt, W_HALF)))
        out_shape.append(jax.ShapeDtypeStruct((b, t, W_HALF), F32))
    hist_rows = 4 * SUBLANES
    return pl.pallas_call(
        functools.partial(_odd_in_kernel, nb=nb, tt=tt, cs=cs, emit_vc=emit_vc),
        grid=grid,
        in_specs=[row((nb, tt, d)), per_b((nb, K_LONG - 1, W_HALF)),
                  _const_spec(gmix.shape), _const_spec(w_in.shape), _const_spec(gvc.shape),
                  _const_spec(ws.shape), _const_spec(bs.shape), _const_spec(conv_w.shape), _const_spec(gd.shape)],
        out_specs=tuple(out_specs),
        out_shape=tuple(out_shape),
        scratch_shapes=[pltpu.VMEM((nb, tt + hist_rows, W_HALF), F32),
                        pltpu.VMEM((SUBLANES - 1, nb, tt + hist_rows - SUBLANES, W_HALF), F32),
                        pltpu.VMEM((nb, tt, W_HALF), F32)],
        compiler_params=pltpu.CompilerParams(dimension_semantics=("arbitrary", "arbitrary"),
                                             vmem_limit_bytes=VMEM_LIMIT),
        name="odd_in",
    )(x, hist_d, gmix, w_in, gvc, ws, bs, conv_w, gd)


def _mix_ffn_kernel(x_ref, oa_ref, ob_ref, hist_ref, wo_ref, g_ref, wu_ref, cw_ref, wd_ref,
                    y_ref, hout_ref, hbuf, abuf, *, nb, tt, n_chunks):
    tm = nb * tt
    pad = SUBLANES
    d_ff = wd_ref.shape[0]
    ch = d_ff // n_chunks

    @pl.when(pl.program_id(1) == 0)
    def _():
        hbuf[:, pad - (K_SHORT - 1):pad, :] = hist_ref[...]

    d = x_ref.shape[-1]
    oa = oa_ref[...].reshape(tm, W_HALF)
    ob = ob_ref[...].reshape(tm, W_HALF)
    x1 = (x_ref[...].reshape(tm, d)
          + jnp.dot(oa, wo_ref[:W_HALF, :], preferred_element_type=F32)
          + jnp.dot(ob, wo_ref[W_HALF:, :], preferred_element_type=F32))
    xn = _rms(x1, g_ref[...]).astype(BF16)

    def conv_up(lo):
        h = jnp.dot(xn, wu_ref[:, lo:lo + ch], preferred_element_type=F32)
        hbuf[:, pad:pad + tt, lo:lo + ch] = h.reshape(nb, tt, ch)
        cw = cw_ref[:, lo:lo + ch]
        return (cw[2:3] * hbuf[:, pad:pad + tt, lo:lo + ch]
                + cw[1:2] * hbuf[:, pad - 1:pad - 1 + tt, lo:lo + ch]
                + cw[0:1] * hbuf[:, pad - 2:pad - 2 + tt, lo:lo + ch]).reshape(tm, ch)

    for c in range(n_chunks):
        gate = conv_up(c * ch)
        val = conv_up(d_ff + c * ch)
        abuf[:, c * ch:(c + 1) * ch] = (gate * jax.nn.sigmoid(gate) * val).astype(BF16)
    y_ref[...] = (x1 + jnp.dot(abuf[...], wd_ref[...], preferred_element_type=F32)).reshape(nb, tt, d)
    tail = hbuf[:, pad + tt - (K_SHORT - 1):pad + tt, :]
    hbuf[:, pad - (K_SHORT - 1):pad, :] = tail
    hout_ref[...] = tail


def _mix_ffn(x, oa, ob, hist_ffn, w_out, g_ffn, w_up, conv_w, w_down, n_chunks=11):
    b, t, d = x.shape
    assert t >= K_SHORT - 1
    nb, tt = _tiles(b, t)
    grid = (b // nb, t // tt)
    two_ff = w_up.shape[1]
    row = lambda shape: pl.BlockSpec(shape, lambda i, j: (i, j, 0))
    per_b = lambda shape: pl.BlockSpec(shape, lambda i, j: (i, 0, 0))
    return pl.pallas_call(
        functools.partial(_mix_ffn_kernel, nb=nb, tt=tt, n_chunks=n_chunks),
        grid=grid,
        in_specs=[row((nb, tt, d)), row((nb, tt, W_HALF)), row((nb, tt, W_HALF)), per_b((nb, K_SHORT - 1, two_ff)),
                  _const_spec(w_out.shape), _const_spec(g_ffn.shape), _const_spec(w_up.shape),
                  _const_spec(conv_w.shape), _const_spec(w_down.shape)],
        out_specs=(row((nb, tt, d)), per_b((nb, K_SHORT - 1, two_ff))),
        out_shape=(jax.ShapeDtypeStruct((b, t, d), F32),
                   jax.ShapeDtypeStruct((b, K_SHORT - 1, two_ff), F32)),
        scratch_shapes=[pltpu.VMEM((nb, tt + SUBLANES, two_ff), F32), pltpu.VMEM((nb * tt, two_ff // 2), BF16)],
        compiler_params=pltpu.CompilerParams(dimension_semantics=("arbitrary", "arbitrary"),
                                             vmem_limit_bytes=VMEM_LIMIT),
        name="mix_ffn",
    )(x, oa, ob, hist_ffn, w_out, g_ffn, w_up, conv_w, w_down)


def _prep_params(g_mix, w_in_even, b_f, g_q, g_k, conv_b, w_out_even, w_in_odd, g_vc, w_s, b_s,
                 conv_d, g_d, w_out_odd, g_ffn, w_up, conv_ffn, w_down):
    depth = g_mix.shape[0]
    off_f = 3 * W_HALF
    head_id = jnp.arange(W_HALF) // D_HEAD
    hmat = jnp.where(head_id[:, None] == head_id[None, :], 1.0 / D_HEAD, 0.0).astype(BF16)
    layers = []
    for l in range(depth):
        i = l // 2
        p = dict(g_mix=g_mix[l][None], g_ffn=g_ffn[l][None], w_up=w_up[l].astype(BF16),
                 conv_ffn=conv_ffn[l], w_down=w_down[l].astype(BF16))
        if l % 2 == 0:
            w = w_in_even[i]
            p.update(
                w_main=jnp.concatenate([w[:, :off_f], w[:, off_f + N_HEADS:]], axis=1).astype(BF16),
                w_f=jnp.pad(w[:, off_f:off_f + N_HEADS], ((0, 0), (0, LANES - N_HEADS))).astype(BF16),
                b_f=jnp.pad(b_f[i], (0, LANES - N_HEADS))[None],
                g_q=jnp.tile(g_q[i], N_HEADS)[None], g_k=jnp.tile(g_k[i], N_HEADS)[None],
                conv_w=conv_b[i], w_out=w_out_even[i].astype(BF16), hmat=hmat)
        else:
            p.update(
                w_in=w_in_odd[i].astype(BF16), g_vc=g_vc[i][None], w_s=w_s[i],
                b_s=jnp.broadcast_to(b_s[i][:, :, None], b_s[i].shape + (LANES,)),
                conv_w=conv_d[i], g_d=g_d[i][None], w_out=w_out_odd[i].astype(BF16))
        layers.append(p)
    return layers


def _trunk(x, layers, hist_b, hist_d, hist_ffn, att_cache):
    b = x.shape[0]
    ks, vs, lfs, hbs, vcs, hds, hfs = [], [], [], [], [], [], []
    if att_cache is not None:
        n_even, _, past = att_cache[0].shape[:3]
        cache_k = att_cache[0].reshape(n_even, b, past, W_HALF)
        cache_v = att_cache[1].reshape(n_even, b, past, W_HALF)
        cache_lf = jnp.pad(att_cache[2], ((0, 0), (0, 0), (0, 0), (0, LANES - N_HEADS)))
    for l, p in enumerate(layers):
        i = l // 2
        if l % 2 == 0:
            if att_cache is None:
                c0 = jnp.zeros((b, 1, LANES), F32)
            else:
                ka_c, va_c, c0 = _cache_prep(cache_k, cache_v, cache_lf, i)
            qa, ka, va, k, v, lf, ob, hb, edges = _even_in(x, hist_b[i], c0, p['g_mix'], p['w_main'], p['w_f'],
                                                           p['b_f'], p['g_q'], p['g_k'], p['conv_w'], p['hmat'])
            t = x.shape[1]
            if att_cache is not None:
                ka = jnp.concatenate([ka_c, ka], axis=2)
                va = jnp.concatenate([va_c, va], axis=2)
                oa = _attention(qa, ka, va)
            else:
                tables = _skip_tables(edges, t // edges.shape[1], min(ATT_BLOCK, t), p['g_q'], p['g_k'])
                oa = _attention(qa, ka, va, tables)
            to_bthd = ((lambda a: jnp.transpose(a, (0, 3, 1, 2))) if k.ndim == 4
                       else (lambda a: a.reshape(b, t, N_HEADS, D_HEAD)))
            ks.append(to_bthd(k))
            vs.append(to_bthd(v))
            lfs.append(lf)
            hbs.append(hb)
        else:
            oa, ob, hd, *vc = _odd_in(x, hist_d[i], p['g_mix'], p['w_in'], p['g_vc'], p['w_s'], p['b_s'],
                                      p['conv_w'], p['g_d'], emit_vc=att_cache is not None)
            vcs.extend(vc)
            hds.append(hd)
        x, hf = _mix_ffn(x, oa, ob, hist_ffn[l], p['w_out'], p['g_ffn'], p['w_up'], p['conv_ffn'], p['w_down'])
        hfs.append(hf)
    return (x, jnp.stack(ks), jnp.stack(vs), jnp.stack(lfs), jnp.stack(hbs),
            jnp.stack(vcs) if vcs else None, jnp.stack(hds), jnp.stack(hfs))


def kernel(x_prompt, x_sample, cache_k, cache_v, cache_logf, state_conv_b, state_conv_d, state_conv_ffn,
           g_mix, w_in_even, b_f, g_q, g_k, conv_b, w_out_even, w_in_odd, g_vc, w_s, b_s,
           conv_d, g_d, w_out_odd, g_ffn, w_up, conv_ffn, w_down):
    layers = _prep_params(g_mix, w_in_even, b_f, g_q, g_k, conv_b, w_out_even, w_in_odd, g_vc, w_s, b_s,
                          conv_d, g_d, w_out_odd, g_ffn, w_up, conv_ffn, w_down)
    b = x_prompt.shape[0]
    depth = g_mix.shape[0]
    n_even, n_odd = (depth + 1) // 2, depth // 2
    zb = jnp.zeros((n_even, b, K_SHORT - 1, W_HALF), F32)
    zd = jnp.zeros((n_odd, b, K_LONG - 1, W_HALF), F32)
    zf = jnp.zeros((depth, b, K_SHORT - 1, w_up.shape[-1]), F32)
    (y_prompt, p_k, p_v, p_logf, p_conv_b, _, p_conv_d, p_conv_ffn) = _trunk(
        x_prompt, layers, zb, zd, zf, None)
    (y_sample, s_k, s_v, s_logf, s_conv_b, s_vc, s_conv_d, s_conv_ffn) = _trunk(
        x_sample, layers, state_conv_b, state_conv_d, state_conv_ffn, (cache_k, cache_v, cache_logf))
    return (y_prompt, y_sample, p_k, p_v, p_logf, p_conv_b, p_conv_d, p_conv_ffn,
            s_k, s_v, s_logf, s_conv_b, s_vc, s_conv_d, s_conv_ffn)
```

```python
import functools
import math

import jax
import jax.numpy as jnp
from jax import lax
from jax.experimental import pallas as pl
from jax.experimental.pallas import tpu as pltpu

F32 = jnp.float32
BF16 = jnp.bfloat16

EPS = 1e-6
LOG2E = math.log2(math.e)
LANES = 128
SUBLANES = 8
N_HEADS = 8
D_HEAD = 64
W_HALF = 512
K_SHORT = 3
K_LONG = 31
GMLP_CHUNK = 128
N_GROUPS = 4
TIME_TILE = 512
ATT_BLOCK = 1024
ATT_WIDE = 2
ATT_CACHE_BLOCK = 4096
NEG_BIG = -1e30
EXP2_UNDERFLOW = 150.0
CONV_ACC_ROWS = 128
VMEM_LIMIT = 56 * 1024 * 1024


def _rms(x, g):
    return x * lax.rsqrt(jnp.mean(x * x, axis=-1, keepdims=True) + EPS) * g


def _split3(x):
    hi = x.astype(BF16).astype(F32)
    r = x - hi
    mid = r.astype(BF16).astype(F32)
    lo = r - mid
    return hi, mid, lo


def _tile_cumsum(lf, carry_ref, nb, tt):
    row = lax.broadcasted_iota(jnp.int32, (tt, tt), 0)
    col = lax.broadcasted_iota(jnp.int32, (tt, tt), 1)
    tri = jnp.where(row >= col, 1.0, 0.0).astype(BF16)
    hi, mid, lo = _split3(lf)
    cat = jnp.concatenate([hi, mid, lo], axis=1).astype(BF16)
    outs = []
    for b in range(nb):
        r = jnp.dot(tri, cat[b * tt:(b + 1) * tt], preferred_element_type=F32)
        c_b = r[:, :LANES] + r[:, LANES:2 * LANES] + r[:, 2 * LANES:] + carry_ref[b]
        carry_ref[b] = c_b[tt - 1:tt, :]
        outs.append(c_b)
    return outs[0] if nb == 1 else jnp.concatenate(outs, axis=0)


def _lane_consts():
    lane = lax.broadcasted_iota(jnp.int32, (1, LANES), 1)
    e = lambda i: jnp.where(lane == i, 1.0, 0.0).astype(F32)
    ones_a = e(D_HEAD) + e(D_HEAD + 1) + e(D_HEAD + 2)
    ones_b = e(D_HEAD + 3) + e(D_HEAD + 4) + e(D_HEAD + 5)
    return lane < D_HEAD, e, ones_a, ones_b


def _augment_heads(a, c2, kind, out_ref, nb, tt):
    head_lanes, e, ones_a, ones_b = _lane_consts()
    tm = nb * tt
    for h in range(N_HEADS):
        pair = a[:, LANES * (h // 2):LANES * (h // 2 + 1)]
        if h % 2:
            pair = pltpu.roll(pair, D_HEAD, axis=1)
        if kind == 'v':
            ext = e(D_HEAD)
        else:
            cb = jnp.broadcast_to(c2[:, h:h + 1], (tm, LANES))
            hi, mid, lo = _split3(cb)
            if kind == 'q':
                ext = hi * e(D_HEAD) + mid * e(D_HEAD + 1) + lo * e(D_HEAD + 2) + ones_b
            else:
                ext = ones_a - (hi * e(D_HEAD + 3) + mid * e(D_HEAD + 4) + lo * e(D_HEAD + 5))
        out_ref[:, h] = jnp.where(head_lanes, pair, ext).astype(BF16).reshape(nb, tt, LANES)


def _log_sigmoid(z):
    return -(jnp.maximum(-z, 0.0) + jnp.log1p(jnp.exp(-jnp.abs(z))))


def _masked_logf(z):
    lane = lax.broadcasted_iota(jnp.int32, (1, LANES), 1)
    return jnp.where(lane < N_HEADS, _log_sigmoid(z), 0.0)


def _even_in_kernel(x_ref, hist_ref, c0_ref, gmix_ref, w_ref, wf_ref, bf_ref, gq_ref, gk_ref, cw_ref, hm_ref,
                    qa_ref, ka_ref, va_ref, k_ref, v_ref, lf_ref, ob_ref, hout_ref, edge_ref,
                    cbuf, ccar, *, nb, tt, kv_time_minor):
    tm = nb * tt
    pad = SUBLANES

    @pl.when(pl.program_id(1) == 0)
    def _():
        cbuf[:, pad - (K_SHORT - 1):pad, :] = hist_ref[...]
        ccar[...] = c0_ref[...]

    x = x_ref[...].reshape(tm, x_ref.shape[-1])
    xn = _rms(x, gmix_ref[...]).astype(BF16)

    def proj(i):
        return jnp.dot(xn, w_ref[:, i * W_HALF:(i + 1) * W_HALF], preferred_element_type=F32)

    hm = hm_ref[...]

    def headnorm(a, g):
        ms = jnp.dot((a * a).astype(BF16), hm, preferred_element_type=F32)
        return a * lax.rsqrt(ms + EPS) * g

    qn = headnorm(proj(0), gq_ref[...])
    kn = headnorm(proj(1), gk_ref[...])
    v = proj(2)
    if kv_time_minor:
        for b in range(nb):
            k_ref[b] = kn[b * tt:(b + 1) * tt, :].T.reshape(N_HEADS, D_HEAD, tt)
            v_ref[b] = v[b * tt:(b + 1) * tt, :].T.reshape(N_HEADS, D_HEAD, tt)
    else:
        k_ref[...] = kn.reshape(nb, tt, W_HALF)
        v_ref[...] = v.reshape(nb, tt, W_HALF)

    z = jnp.dot(xn, wf_ref[...], preferred_element_type=F32) + bf_ref[...]
    lf = _masked_logf(z)
    lf_ref[...] = lf[:, :N_HEADS].reshape(nb, tt, N_HEADS)
    c2 = _tile_cumsum(lf, ccar, nb, tt) * LOG2E
    for b in range(nb):
        edge_ref[b, 0, 0:1, :] = c2[b * tt:b * tt + 1, :]
        edge_ref[b, 0, 1:2, :] = c2[(b + 1) * tt - 1:(b + 1) * tt, :]
        edge_ref[b, 0, 2:, :] = jnp.zeros((SUBLANES - 2, LANES), F32)

    _augment_heads(qn * (LOG2E * D_HEAD ** -0.5), c2, 'q', qa_ref, nb, tt)
    _augment_heads(kn, c2, 'k', ka_ref, nb, tt)
    _augment_heads(v, None, 'v', va_ref, nb, tt)

    bg = proj(3)
    cgx = proj(4) * proj(5)
    cbuf[:, pad:pad + tt, :] = cgx.reshape(nb, tt, W_HALF)
    cw = cw_ref[...]
    cx = (cw[2:3] * cbuf[:, pad:pad + tt, :] + cw[1:2] * cbuf[:, pad - 1:pad - 1 + tt, :]
          + cw[0:1] * cbuf[:, pad - 2:pad - 2 + tt, :])
    ob_ref[...] = (bg.reshape(nb, tt, W_HALF) * cx).astype(BF16)
    tail = cbuf[:, pad + tt - (K_SHORT - 1):pad + tt, :]
    cbuf[:, pad - (K_SHORT - 1):pad, :] = tail
    hout_ref[...] = tail


def _const_spec(shape):
    return pl.BlockSpec(shape, lambda *_: (0,) * len(shape), pipeline_mode=pl.Buffered(1))


def _layer_weight(w):
    stacked, layer = w
    spec = pl.BlockSpec((None,) + stacked.shape[1:], lambda *_: (layer,) + (0,) * (stacked.ndim - 1),
                        pipeline_mode=pl.Buffered(1))
    return stacked, spec


def _tiles(b, t):
    if t >= TIME_TILE:
        assert t % TIME_TILE == 0
        return 1, TIME_TILE
    assert t % (2 * SUBLANES) == 0
    return b, t


def _even_in(x, hist_b, c0, gmix, w_main, w_f, b_f, gq, gk, conv_w, hmat):
    b, t, d = x.shape
    assert t >= K_SHORT - 1
    nb, tt = _tiles(b, t)
    grid = (b // nb, t // tt)
    row = lambda shape: pl.BlockSpec(shape, lambda i, j: (i, j, 0))
    per_b = lambda shape: pl.BlockSpec(shape, lambda i, j: (i, 0, 0))
    heads = pl.BlockSpec((nb, N_HEADS, tt, LANES), lambda i, j: (i, 0, j, 0))
    kv_time_minor = tt % LANES == 0
    if kv_time_minor:
        kv_shape = jax.ShapeDtypeStruct((b, N_HEADS, D_HEAD, t), F32)
        kv_spec = pl.BlockSpec((nb, N_HEADS, D_HEAD, tt), lambda i, j: (i, 0, 0, j))
    else:
        kv_shape = jax.ShapeDtypeStruct((b, t, W_HALF), F32)
        kv_spec = row((nb, tt, W_HALF))
    out_shape = (
        jax.ShapeDtypeStruct((b, N_HEADS, t, LANES), BF16),
        jax.ShapeDtypeStruct((b, N_HEADS, t, LANES), BF16),
        jax.ShapeDtypeStruct((b, N_HEADS, t, LANES), BF16),
        kv_shape,
        kv_shape,
        jax.ShapeDtypeStruct((b, t, N_HEADS), F32),
        jax.ShapeDtypeStruct((b, t, W_HALF), BF16),
        jax.ShapeDtypeStruct((b, K_SHORT - 1, W_HALF), F32),
        jax.ShapeDtypeStruct((b, t // tt, SUBLANES, LANES), F32),
    )
    return pl.pallas_call(
        functools.partial(_even_in_kernel, nb=nb, tt=tt, kv_time_minor=kv_time_minor),
        grid=grid,
        in_specs=[row((nb, tt, d)), per_b((nb, K_SHORT - 1, W_HALF)), per_b((nb, 1, LANES)),
                  _const_spec(gmix.shape), _const_spec(w_main.shape), _const_spec(w_f.shape),
                  _const_spec(b_f.shape), _const_spec(gq.shape), _const_spec(gk.shape),
                  _const_spec(conv_w.shape), _const_spec(hmat.shape)],
        out_specs=(heads, heads, heads, kv_spec, kv_spec, row((nb, tt, N_HEADS)),
                   row((nb, tt, W_HALF)), per_b((nb, K_SHORT - 1, W_HALF)),
                   pl.BlockSpec((nb, 1, SUBLANES, LANES), lambda i, j: (i, j, 0, 0))),
        out_shape=out_shape,
        scratch_shapes=[pltpu.VMEM((nb, tt + SUBLANES, W_HALF), F32), pltpu.VMEM((nb, 1, LANES), F32)],
        compiler_params=pltpu.CompilerParams(dimension_semantics=("arbitrary", "arbitrary"),
                                             vmem_limit_bytes=VMEM_LIMIT),
        name="even_in",
    )(x, hist_b, c0, gmix, w_main, w_f, b_f, gq, gk, conv_w, hmat)


def _cache_prep_kernel(k_ref, v_ref, lf_ref, ka_ref, va_ref, ctot_ref, ccar, *, tt):
    @pl.when(pl.program_id(1) == 0)
    def _():
        ccar[...] = jnp.zeros_like(ccar)

    def time_major(ref):
        pairs = [ref[0, 2 * p:2 * p + 2].reshape(LANES, tt).T for p in range(N_HEADS // 2)]
        return jnp.concatenate(pairs, axis=1)

    lf = jnp.concatenate([lf_ref[0], jnp.zeros((LANES - N_HEADS, tt), F32)], axis=0).T
    c2 = _tile_cumsum(lf, ccar, 1, tt) * LOG2E
    _augment_heads(time_major(k_ref), c2, 'k', ka_ref, 1, tt)
    _augment_heads(time_major(v_ref), None, 'v', va_ref, 1, tt)
    ctot_ref[...] = ccar[...]


def _cache_prep(k, v, lf, layer):
    _, b, _, _, t = k.shape
    tt = min(t, TIME_TILE)
    assert t % tt == 0 and tt % LANES == 0
    kv_in = pl.BlockSpec((None, 1, N_HEADS, D_HEAD, tt), lambda i, j: (layer, i, 0, 0, j))
    lf_in = pl.BlockSpec((None, 1, N_HEADS, tt), lambda i, j: (layer, i, 0, j))
    heads = pl.BlockSpec((1, N_HEADS, tt, LANES), lambda i, j: (i, 0, j, 0))
    return pl.pallas_call(
        functools.partial(_cache_prep_kernel, tt=tt),
        grid=(b, t // tt),
        in_specs=[kv_in, kv_in, lf_in],
        out_specs=(heads, heads, pl.BlockSpec((1, 1, LANES), lambda i, j: (i, 0, 0))),
        out_shape=(jax.ShapeDtypeStruct((b, N_HEADS, t, LANES), BF16),
                   jax.ShapeDtypeStruct((b, N_HEADS, t, LANES), BF16),
                   jax.ShapeDtypeStruct((b, 1, LANES), F32)),
        scratch_shapes=[pltpu.VMEM((1, 1, LANES), F32)],
        compiler_params=pltpu.CompilerParams(dimension_semantics=("arbitrary", "arbitrary"),
                                             vmem_limit_bytes=VMEM_LIMIT),
        name="cache_prep",
    )(k, v, lf)


def _attn_kernel(thr_ref, klast_ref, qa_ref, kf_ref, vf_ref, *rest, tq, tk, past, skip, split, wide):
    o_ref = rest[-1]
    i = pl.program_id(2)
    diag_off = past + i * tq
    n_full = diag_off // tk
    kd_ref, vd_ref, doff = (rest[0], rest[1], 0) if split else (kf_ref, vf_ref, pl.multiple_of(diag_off, tq))
    nt = (((1,), (1,)), ((), ()))
    qs = [qa_ref[0, hh] for hh in range(2)]
    heads = range(2)

    def scores(hh, k_ref, off, size):
        return lax.dot_general(qs[hh], k_ref[0, hh, pl.ds(off, size), :], nt, preferred_element_type=F32)

    def consume(state, hh, s, v_ref, off, size):
        m, acc = state
        m_new = jnp.maximum(m, jnp.max(s, axis=-1, keepdims=True))
        p = jnp.exp2(s - m_new).astype(BF16)
        pv = jnp.dot(p, v_ref[0, hh, pl.ds(off, size), :], preferred_element_type=F32)
        return m_new, jnp.exp2(m - m_new) * acc + pv

    init = tuple((jnp.full((tq, 1), NEG_BIG, F32), jnp.zeros((tq, LANES), F32)) for _ in heads)

    def body(j, states):
        off = pl.multiple_of(j * tk, tk)
        return tuple(consume(states[hh], hh, scores(hh, kf_ref, off, tk), vf_ref, off, tk) for hh in heads)

    first = 0
    if skip:
        def skippable(hh):
            r = (pl.program_id(0) * pl.num_programs(1) + pl.program_id(1)) * 2 + hh
            thr = thr_ref[r, i]
            return lax.fori_loop(0, n_full, lambda j, n: n + (klast_ref[r, j] >= thr).astype(jnp.int32), 0)

        first = jnp.minimum(skippable(0), skippable(1))
    if wide > 1:
        def wide_body(j, states):
            off = pl.multiple_of((first + wide * j) * tk, tk)
            return tuple(consume(states[hh], hh, scores(hh, kf_ref, off, wide * tk), vf_ref, off, wide * tk)
                         for hh in heads)

        n_wide = (n_full - first) // wide
        init = lax.fori_loop(0, n_wide, wide_body, init)
        first = first + wide * n_wide
    states = lax.fori_loop(first, n_full, body, init)

    mask = lax.broadcasted_iota(jnp.int32, (tq, tq), 0) >= lax.broadcasted_iota(jnp.int32, (tq, tq), 1)
    outs = []
    for hh in heads:
        s = jnp.where(mask, scores(hh, kd_ref, doff, tq), NEG_BIG)
        _, acc = consume(states[hh], hh, s, vd_ref, doff, tq)
        outs.append(acc / acc[:, D_HEAD:D_HEAD + 1])
    lane = lax.broadcasted_iota(jnp.int32, (1, LANES), 1)
    o_ref[0] = jnp.where(lane < D_HEAD, outs[0], pltpu.roll(outs[1], D_HEAD, axis=1)).astype(BF16)


def _skip_tables(edges, tile, block, g_q, g_k):
    b = edges.shape[0]
    per = block // tile
    qk_bound = 1.02 * D_HEAD ** 0.5 * LOG2E * jnp.max(jnp.abs(g_q)) * jnp.max(jnp.abs(g_k))
    margin = 2.0 * qk_bound + (EXP2_UNDERFLOW + 1.0)
    q_first = edges[:, 0::per, 0, :N_HEADS]
    k_last = edges[:, per - 1::per, 1, :N_HEADS]
    to_rows = lambda a: jnp.transpose(a, (0, 2, 1)).reshape(b * N_HEADS, -1)
    return to_rows(q_first + margin), to_rows(k_last)


def _attention(qa, ka, va, skip_tables=None, cache=None):
    b, h, t_q, _ = qa.shape
    split = cache is not None
    kf, vf = cache if split else (ka, va)
    t_f = kf.shape[2]
    past = t_f if split else 0
    tq = min(ATT_BLOCK, t_q)
    tk = math.gcd(past, ATT_CACHE_BLOCK) if split else tq
    assert t_q % tq == 0 and past % tk == 0 and (tq % tk == 0 or t_q == tq)
    skip = skip_tables is not None
    if skip:
        assert not split and h == N_HEADS
        thr, k_last = skip_tables
    else:
        thr = k_last = jnp.zeros((1, 1), F32)
    kv_bytes = 2 * t_f * LANES * 2
    kv_mode = dict(pipeline_mode=pl.Buffered(1)) if 4 * kv_bytes > VMEM_LIMIT // 4 else {}
    kv_spec = pl.BlockSpec((1, 2, t_f, LANES), lambda i, p, j: (i, p, 0, 0), **kv_mode)
    q_spec = pl.BlockSpec((1, 2, tq, LANES), lambda i, p, j: (i, p, j, 0))
    smem = pl.BlockSpec(memory_space=pltpu.SMEM)
    own_specs, own_args = ([q_spec, q_spec], [ka, va]) if split else ([], [])
    return pl.pallas_call(
        functools.partial(_attn_kernel, tq=tq, tk=tk, past=past, skip=skip, split=split,
                          wide=1 if split else ATT_WIDE),
        grid=(b, h // 2, t_q // tq),
        in_specs=[smem, smem, q_spec, kv_spec, kv_spec] + own_specs,
        out_specs=pl.BlockSpec((1, tq, LANES), lambda i, p, j: (i, j, p)),
        out_shape=jax.ShapeDtypeStruct((b, t_q, h * D_HEAD), BF16),
        compiler_params=pltpu.CompilerParams(dimension_semantics=("arbitrary", "arbitrary", "arbitrary"),
                                             vmem_limit_bytes=VMEM_LIMIT),
        name="fox_attention",
    )(thr, k_last, qa, kf, vf, *own_args)


def _gelu_tanh(x):
    return 0.5 * x * (1.0 + jnp.tanh(math.sqrt(2.0 / math.pi) * (x + 0.044715 * (x * x * x))))


def _long_conv(dbuf, phase, cw_ref, cd_ref, nb, tt, first):
    rows_kept = tt + first + K_LONG - 1 - SUBLANES
    for r in range(1, SUBLANES):
        phase[r - 1, :, :rows_kept, :] = dbuf[:, r:r + rows_kept, :]
    rb = CONV_ACC_ROWS // nb
    for cb in range(W_HALF // LANES):
        cols = slice(cb * LANES, (cb + 1) * LANES)
        w_cols = cw_ref[:, cols]
        for r0 in range(0, tt, rb):
            acc = None
            for k in range(K_LONG):
                r, a = (first + k) % SUBLANES, (first + k) // SUBLANES * SUBLANES
                src = dbuf if r == 0 else phase.at[r - 1]
                term = w_cols[k:k + 1] * src[:, a + r0:a + r0 + rb, cols]
                acc = term if acc is None else acc + term
            cd_ref[:, r0:r0 + rb, cols] = acc


def _odd_in_kernel(x_ref, hist_ref, gmix_ref, w_ref, gvc_ref, ws_ref, bs_ref, cw_ref, gd_ref,
                   oc_ref, od_ref, hout_ref, *rest, nb, tt, cs, emit_vc):
    vc_ref = rest[0] if emit_vc else None
    dbuf, phase, cdbuf = rest[-3:]
    tm = nb * tt
    pad = 4 * SUBLANES
    nh = K_LONG - 1

    @pl.when(pl.program_id(1) == 0)
    def _():
        dbuf[:, pad - nh:pad, :] = hist_ref[...]

    x = x_ref[...].reshape(tm, x_ref.shape[-1])
    xn = _rms(x, gmix_ref[...]).astype(BF16)

    def proj(i):
        return jnp.dot(xn, w_ref[:, i * W_HALF:(i + 1) * W_HALF], preferred_element_type=F32)

    u = _gelu_tanh(proj(0))
    vc = _rms(_gelu_tanh(proj(1)), gvc_ref[...])
    if emit_vc:
        vc_ref[...] = vc.reshape(nb, tt, W_HALF)
    vcb = vc.astype(BF16)

    row = lax.broadcasted_iota(jnp.int32, (cs, cs), 0)
    col = lax.broadcasted_iota(jnp.int32, (cs, cs), 1)
    per_b = tt // cs
    for g in range(N_GROUPS):
        wsg = jnp.where(row >= col, ws_ref[g], 0.0).astype(BF16)
        for ci in range(tm // cs):
            r0 = ci * cs
            gate = jnp.dot(wsg, vcb[r0:r0 + cs, g * LANES:(g + 1) * LANES], preferred_element_type=F32) + bs_ref[g]
            oc = u[r0:r0 + cs, g * LANES:(g + 1) * LANES] * gate
            t0 = (ci % per_b) * cs
            oc_ref[ci // per_b, t0:t0 + cs, g * LANES:(g + 1) * LANES] = oc.astype(BF16)

    glu = proj(2) * jax.nn.sigmoid(proj(3))
    dbuf[:, pad:pad + tt, :] = glu.reshape(nb, tt, W_HALF)
    _long_conv(dbuf, phase, cw_ref, cdbuf, nb, tt, pad - nh)
    y = _rms(cdbuf[...].reshape(tm, W_HALF), gd_ref[...])
    od_ref[...] = (y * jax.nn.sigmoid(y)).astype(BF16).reshape(nb, tt, W_HALF)
    tail = dbuf[:, pad + tt - nh:pad + tt, :]
    dbuf[:, pad - nh:pad, :] = tail
    hout_ref[...] = tail


def _odd_in(x, hist_d, gmix, w_in, gvc, ws, bs, conv_w, gd, emit_vc):
    b, t, d = x.shape
    assert t >= K_LONG - 1
    nb, tt = _tiles(b, t)
    cs = min(GMLP_CHUNK, tt)
    assert tt % cs == 0 and CONV_ACC_ROWS % nb == 0 and tt % (CONV_ACC_ROWS // nb) == 0
    grid = (b // nb, t // tt)
    row = lambda shape: pl.BlockSpec(shape, lambda i, j: (i, j, 0))
    per_b = lambda shape: pl.BlockSpec(shape, lambda i, j: (i, 0, 0))
    ws = ws[:, :cs, :cs]
    bs = bs[:, :cs, :]
    w_in, w_in_spec = _layer_weight(w_in)
    out_specs = [row((nb, tt, W_HALF)), row((nb, tt, W_HALF)), per_b((nb, K_LONG - 1, W_HALF))]
    out_shape = [jax.ShapeDtypeStruct((b, t, W_HALF), BF16), jax.ShapeDtypeStruct((b, t, W_HALF), BF16),
                 jax.ShapeDtypeStruct((b, K_LONG - 1, W_HALF), F32)]
    if emit_vc:
        out_specs.append(row((nb, tt, W_HALF)))
        out_shape.append(jax.ShapeDtypeStruct((b, t, W_HALF), F32))
    hist_rows = 4 * SUBLANES
    return pl.pallas_call(
        functools.partial(_odd_in_kernel, nb=nb, tt=tt, cs=cs, emit_vc=emit_vc),
        grid=grid,
        in_specs=[row((nb, tt, d)), per_b((nb, K_LONG - 1, W_HALF)),
                  _const_spec(gmix.shape), w_in_spec, _const_spec(gvc.shape),
                  _const_spec(ws.shape), _const_spec(bs.shape), _const_spec(conv_w.shape), _const_spec(gd.shape)],
        out_specs=tuple(out_specs),
        out_shape=tuple(out_shape),
        scratch_shapes=[pltpu.VMEM((nb, tt + hist_rows, W_HALF), F32),
                        pltpu.VMEM((SUBLANES - 1, nb, tt + hist_rows - SUBLANES, W_HALF), F32),
                        pltpu.VMEM((nb, tt, W_HALF), F32)],
        compiler_params=pltpu.CompilerParams(dimension_semantics=("arbitrary", "arbitrary"),
                                             vmem_limit_bytes=VMEM_LIMIT),
        name="odd_in",
    )(x, hist_d, gmix, w_in, gvc, ws, bs, conv_w, gd)


def _mix_ffn_kernel(x_ref, oa_ref, ob_ref, hist_ref, wo_ref, g_ref, wu_ref, cw_ref, wd_ref,
                    y_ref, hout_ref, hbuf, abuf, *, nb, tt):
    tm = nb * tt
    pad = SUBLANES
    d_ff = wd_ref.shape[0]

    @pl.when(pl.program_id(1) == 0)
    def _():
        hbuf[:, pad - (K_SHORT - 1):pad, :] = hist_ref[...]

    d = x_ref.shape[-1]
    oa = oa_ref[...].reshape(tm, W_HALF)
    ob = ob_ref[...].reshape(tm, W_HALF)
    x1 = (x_ref[...].reshape(tm, d)
          + jnp.dot(oa, wo_ref[:W_HALF, :], preferred_element_type=F32)
          + jnp.dot(ob, wo_ref[W_HALF:, :], preferred_element_type=F32))
    xn = _rms(x1, g_ref[...]).astype(BF16)

    def conv_up(lo):
        h = jnp.dot(xn, wu_ref[:, lo:lo + d_ff], preferred_element_type=F32)
        hbuf[:, pad:pad + tt, lo:lo + d_ff] = h.reshape(nb, tt, d_ff)
        cw = cw_ref[:, lo:lo + d_ff]
        return (cw[2:3] * hbuf[:, pad:pad + tt, lo:lo + d_ff]
                + cw[1:2] * hbuf[:, pad - 1:pad - 1 + tt, lo:lo + d_ff]
                + cw[0:1] * hbuf[:, pad - 2:pad - 2 + tt, lo:lo + d_ff]).reshape(tm, d_ff)

    gate = conv_up(0)
    val = conv_up(d_ff)
    abuf[...] = (gate * jax.nn.sigmoid(gate) * val).astype(BF16)
    y_ref[...] = (x1 + jnp.dot(abuf[...], wd_ref[...], preferred_element_type=F32)).reshape(nb, tt, d)
    tail = hbuf[:, pad + tt - (K_SHORT - 1):pad + tt, :]
    hbuf[:, pad - (K_SHORT - 1):pad, :] = tail
    hout_ref[...] = tail


def _mix_ffn(x, oa, ob, hist_ffn, w_out, g_ffn, w_up, conv_w, w_down):
    b, t, d = x.shape
    assert t >= K_SHORT - 1
    nb, tt = _tiles(b, t)
    grid = (b // nb, t // tt)
    (w_out, w_out_spec), (w_up, w_up_spec), (w_down, w_down_spec) = map(_layer_weight, (w_out, w_up, w_down))
    two_ff = w_up.shape[-1]
    row = lambda shape: pl.BlockSpec(shape, lambda i, j: (i, j, 0))
    per_b = lambda shape: pl.BlockSpec(shape, lambda i, j: (i, 0, 0))
    return pl.pallas_call(
        functools.partial(_mix_ffn_kernel, nb=nb, tt=tt),
        grid=grid,
        in_specs=[row((nb, tt, d)), row((nb, tt, W_HALF)), row((nb, tt, W_HALF)), per_b((nb, K_SHORT - 1, two_ff)),
                  w_out_spec, _const_spec(g_ffn.shape), w_up_spec, _const_spec(conv_w.shape), w_down_spec],
        out_specs=(row((nb, tt, d)), per_b((nb, K_SHORT - 1, two_ff))),
        out_shape=(jax.ShapeDtypeStruct((b, t, d), F32),
                   jax.ShapeDtypeStruct((b, K_SHORT - 1, two_ff), F32)),
        scratch_shapes=[pltpu.VMEM((nb, tt + SUBLANES, two_ff), F32), pltpu.VMEM((nb * tt, two_ff // 2), BF16)],
        compiler_params=pltpu.CompilerParams(dimension_semantics=("arbitrary", "arbitrary"),
                                             vmem_limit_bytes=VMEM_LIMIT),
        name="mix_ffn",
    )(x, oa, ob, hist_ffn, w_out, g_ffn, w_up, conv_w, w_down)


def _prep_params(g_mix, w_in_even, b_f, g_q, g_k, conv_b, w_out_even, w_in_odd, g_vc, w_s, b_s,
                 conv_d, g_d, w_out_odd, g_ffn, w_up, conv_ffn, w_down):
    depth = g_mix.shape[0]
    off_f = 3 * W_HALF
    head_id = jnp.arange(W_HALF) // D_HEAD
    hmat = jnp.where(head_id[:, None] == head_id[None, :], 1.0 / D_HEAD, 0.0).astype(BF16)
    w_up, w_down, w_in_odd, w_out_even, w_out_odd = (
        w.astype(BF16) for w in (w_up, w_down, w_in_odd, w_out_even, w_out_odd))
    layers = []
    for l in range(depth):
        i = l // 2
        p = dict(g_mix=g_mix[l][None], g_ffn=g_ffn[l][None], w_up=(w_up, l),
                 conv_ffn=conv_ffn[l], w_down=(w_down, l))
        if l % 2 == 0:
            w = w_in_even[i]
            p.update(
                w_main=jnp.concatenate([w[:, :off_f], w[:, off_f + N_HEADS:]], axis=1).astype(BF16),
                w_f=jnp.pad(w[:, off_f:off_f + N_HEADS], ((0, 0), (0, LANES - N_HEADS))).astype(BF16),
                b_f=jnp.pad(b_f[i], (0, LANES - N_HEADS))[None],
                g_q=jnp.tile(g_q[i], N_HEADS)[None], g_k=jnp.tile(g_k[i], N_HEADS)[None],
                conv_w=conv_b[i], w_out=(w_out_even, i), hmat=hmat)
        else:
            p.update(
                w_in=(w_in_odd, i), g_vc=g_vc[i][None], w_s=w_s[i],
                b_s=jnp.broadcast_to(b_s[i][:, :, None], b_s[i].shape + (LANES,)),
                conv_w=conv_d[i], g_d=g_d[i][None], w_out=(w_out_odd, i))
        layers.append(p)
    return layers


def _trunk(x, layers, hist_b, hist_d, hist_ffn, att_cache):
    b = x.shape[0]
    ks, vs, lfs, hbs, vcs, hds, hfs = [], [], [], [], [], [], []
    if att_cache is not None:
        cache_k = jnp.transpose(att_cache[0], (0, 1, 3, 4, 2))
        cache_v = jnp.transpose(att_cache[1], (0, 1, 3, 4, 2))
        cache_lf = jnp.transpose(att_cache[2], (0, 1, 3, 2))
    for l, p in enumerate(layers):
        i = l // 2
        if l % 2 == 0:
            if att_cache is None:
                c0 = jnp.zeros((b, 1, LANES), F32)
            else:
                ka_c, va_c, c0 = _cache_prep(cache_k, cache_v, cache_lf, i)
            qa, ka, va, k, v, lf, ob, hb, edges = _even_in(x, hist_b[i], c0, p['g_mix'], p['w_main'], p['w_f'],
                                                           p['b_f'], p['g_q'], p['g_k'], p['conv_w'], p['hmat'])
            t = x.shape[1]
            if att_cache is not None:
                oa = _attention(qa, ka, va, cache=(ka_c, va_c))
            else:
                tables = _skip_tables(edges, t // edges.shape[1], min(ATT_BLOCK, t), p['g_q'], p['g_k'])
                oa = _attention(qa, ka, va, tables)
            to_bthd = ((lambda a: jnp.transpose(a, (0, 3, 1, 2))) if k.ndim == 4
                       else (lambda a: a.reshape(b, t, N_HEADS, D_HEAD)))
            ks.append(to_bthd(k))
            vs.append(to_bthd(v))
            lfs.append(lf)
            hbs.append(hb)
        else:
            oa, ob, hd, *vc = _odd_in(x, hist_d[i], p['g_mix'], p['w_in'], p['g_vc'], p['w_s'], p['b_s'],
                                      p['conv_w'], p['g_d'], emit_vc=att_cache is not None)
            vcs.extend(vc)
            hds.append(hd)
        x, hf = _mix_ffn(x, oa, ob, hist_ffn[l], p['w_out'], p['g_ffn'], p['w_up'], p['conv_ffn'], p['w_down'])
        hfs.append(hf)
    return (x, jnp.stack(ks), jnp.stack(vs), jnp.stack(lfs), jnp.stack(hbs),
            jnp.stack(vcs) if vcs else None, jnp.stack(hds), jnp.stack(hfs))


def kernel(x_prompt, x_sample, cache_k, cache_v, cache_logf, state_conv_b, state_conv_d, state_conv_ffn,
           g_mix, w_in_even, b_f, g_q, g_k, conv_b, w_out_even, w_in_odd, g_vc, w_s, b_s,
           conv_d, g_d, w_out_odd, g_ffn, w_up, conv_ffn, w_down):
    layers = _prep_params(g_mix, w_in_even, b_f, g_q, g_k, conv_b, w_out_even, w_in_odd, g_vc, w_s, b_s,
                          conv_d, g_d, w_out_odd, g_ffn, w_up, conv_ffn, w_down)
    b = x_prompt.shape[0]
    depth = g_mix.shape[0]
    n_even, n_odd = (depth + 1) // 2, depth // 2
    zb = jnp.zeros((n_even, b, K_SHORT - 1, W_HALF), F32)
    zd = jnp.zeros((n_odd, b, K_LONG - 1, W_HALF), F32)
    zf = jnp.zeros((depth, b, K_SHORT - 1, w_up.shape[-1]), F32)
    (y_prompt, p_k, p_v, p_logf, p_conv_b, _, p_conv_d, p_conv_ffn) = _trunk(
        x_prompt, layers, zb, zd, zf, None)
    (y_sample, s_k, s_v, s_logf, s_conv_b, s_vc, s_conv_d, s_conv_ffn) = _trunk(
        x_sample, layers, state_conv_b, state_conv_d, state_conv_ffn, (cache_k, cache_v, cache_logf))
    return (y_prompt, y_sample, p_k, p_v, p_logf, p_conv_b, p_conv_d, p_conv_ffn,
            s_k, s_v, s_logf, s_conv_b, s_vc, s_conv_d, s_conv_ffn)
```

```python
import functools
import math

import jax
import jax.numpy as jnp
from jax import lax
from jax.experimental import pallas as pl
from jax.experimental.pallas import tpu as pltpu

F32 = jnp.float32
BF16 = jnp.bfloat16

EPS = 1e-6
LOG2E = math.log2(math.e)
LANES = 128
SUBLANES = 8
N_HEADS = 8
D_HEAD = 64
W_HALF = 512
K_SHORT = 3
K_LONG = 31
GMLP_CHUNK = 128
N_GROUPS = 4
TIME_TILE = 512
ATT_BLOCK = 1024
ATT_WIDE = 2
ATT_CACHE_BLOCK = 4096
NEG_BIG = -1e30
EXP2_UNDERFLOW = 150.0
CONV_ACC_ROWS = 128
VMEM_LIMIT = 56 * 1024 * 1024


def _rms(x, g):
    return x * lax.rsqrt(jnp.mean(x * x, axis=-1, keepdims=True) + EPS) * g


def _split3(x):
    hi = x.astype(BF16).astype(F32)
    r = x - hi
    mid = r.astype(BF16).astype(F32)
    lo = r - mid
    return hi, mid, lo


def _tile_cumsum(lf, carry_ref, nb, tt):
    row = lax.broadcasted_iota(jnp.int32, (tt, tt), 0)
    col = lax.broadcasted_iota(jnp.int32, (tt, tt), 1)
    tri = jnp.where(row >= col, 1.0, 0.0).astype(BF16)
    hi, mid, lo = _split3(lf)
    cat = jnp.concatenate([hi, mid, lo], axis=1).astype(BF16)
    outs = []
    for b in range(nb):
        r = jnp.dot(tri, cat[b * tt:(b + 1) * tt], preferred_element_type=F32)
        c_b = r[:, :LANES] + r[:, LANES:2 * LANES] + r[:, 2 * LANES:] + carry_ref[b]
        carry_ref[b] = c_b[tt - 1:tt, :]
        outs.append(c_b)
    return outs[0] if nb == 1 else jnp.concatenate(outs, axis=0)


def _lane_consts():
    lane = lax.broadcasted_iota(jnp.int32, (1, LANES), 1)
    e = lambda i: jnp.where(lane == i, 1.0, 0.0).astype(F32)
    ones_a = e(D_HEAD) + e(D_HEAD + 1) + e(D_HEAD + 2)
    ones_b = e(D_HEAD + 3) + e(D_HEAD + 4) + e(D_HEAD + 5)
    return lane < D_HEAD, e, ones_a, ones_b


def _augment_heads(a, c2, kind, out_ref, nb, tt, split_once=False):
    head_lanes, e, ones_a, ones_b = _lane_consts()
    tm = nb * tt
    bcast = lambda x, h: jnp.broadcast_to(x[:, h:h + 1], (tm, LANES))
    pieces = _split3(c2) if split_once and kind != 'v' else None
    for h in range(N_HEADS):
        pair = a[:, LANES * (h // 2):LANES * (h // 2 + 1)]
        if h % 2:
            pair = pltpu.roll(pair, D_HEAD, axis=1)
        if kind == 'v':
            ext = e(D_HEAD)
        else:
            hi, mid, lo = [bcast(x, h) for x in pieces] if pieces else _split3(bcast(c2, h))
            if kind == 'q':
                ext = hi * e(D_HEAD) + mid * e(D_HEAD + 1) + lo * e(D_HEAD + 2) + ones_b
            else:
                ext = ones_a - (hi * e(D_HEAD + 3) + mid * e(D_HEAD + 4) + lo * e(D_HEAD + 5))
        out_ref[:, h] = jnp.where(head_lanes, pair, ext).astype(BF16).reshape(nb, tt, LANES)


def _log_sigmoid(z):
    return -(jnp.maximum(-z, 0.0) + jnp.log1p(jnp.exp(-jnp.abs(z))))


def _masked_logf(z):
    lane = lax.broadcasted_iota(jnp.int32, (1, LANES), 1)
    return jnp.where(lane < N_HEADS, _log_sigmoid(z), 0.0)


def _even_in_kernel(x_ref, hist_ref, c0_ref, gmix_ref, w_ref, wf_ref, bf_ref, gq_ref, gk_ref, cw_ref, hm_ref,
                    *rest, nb, tt, kv_time_minor, n_prev):
    kprev_ref, vprev_ref = rest[:2] if n_prev else (None, None)
    (qa_ref, ka_ref, va_ref, k_ref, v_ref, lf_ref, ob_ref, hout_ref, edge_ref, cbuf, ccar) = rest[2 if n_prev else 0:]
    tm = nb * tt
    pad = SUBLANES

    @pl.when(pl.program_id(1) == 0)
    def _():
        cbuf[:, pad - (K_SHORT - 1):pad, :] = hist_ref[...]
        ccar[...] = c0_ref[...]

    x = x_ref[...].reshape(tm, x_ref.shape[-1])
    xn = _rms(x, gmix_ref[...]).astype(BF16)

    def proj(i):
        return jnp.dot(xn, w_ref[:, i * W_HALF:(i + 1) * W_HALF], preferred_element_type=F32)

    hm = hm_ref[...]

    def headnorm(a, g):
        ms = jnp.dot((a * a).astype(BF16), hm, preferred_element_type=F32)
        return a * lax.rsqrt(ms + EPS) * g

    qn = headnorm(proj(0), gq_ref[...])
    kn = headnorm(proj(1), gk_ref[...])
    v = proj(2)
    if kv_time_minor:
        if n_prev:
            k_ref[:n_prev] = kprev_ref[...]
            v_ref[:n_prev] = vprev_ref[...]
        for b in range(nb):
            k_ref[n_prev, b] = kn[b * tt:(b + 1) * tt, :].T.reshape(N_HEADS, D_HEAD, tt)
            v_ref[n_prev, b] = v[b * tt:(b + 1) * tt, :].T.reshape(N_HEADS, D_HEAD, tt)
    else:
        k_ref[...] = kn.reshape(nb, tt, W_HALF)
        v_ref[...] = v.reshape(nb, tt, W_HALF)

    z = jnp.dot(xn, wf_ref[...], preferred_element_type=F32) + bf_ref[...]
    lf = _masked_logf(z)
    lf_ref[...] = lf[:, :N_HEADS].reshape(nb, tt, N_HEADS)
    c2 = _tile_cumsum(lf, ccar, nb, tt) * LOG2E
    for b in range(nb):
        edge_ref[b, 0, 0:1, :] = c2[b * tt:b * tt + 1, :]
        edge_ref[b, 0, 1:2, :] = c2[(b + 1) * tt - 1:(b + 1) * tt, :]
        edge_ref[b, 0, 2:, :] = jnp.zeros((SUBLANES - 2, LANES), F32)

    _augment_heads(qn * (LOG2E * D_HEAD ** -0.5), c2, 'q', qa_ref, nb, tt, split_once=True)
    _augment_heads(kn, c2, 'k', ka_ref, nb, tt, split_once=True)
    _augment_heads(v, None, 'v', va_ref, nb, tt)

    bg = proj(3)
    cgx = proj(4) * proj(5)
    cbuf[:, pad:pad + tt, :] = cgx.reshape(nb, tt, W_HALF)
    cw = cw_ref[...]
    cx = (cw[2:3] * cbuf[:, pad:pad + tt, :] + cw[1:2] * cbuf[:, pad - 1:pad - 1 + tt, :]
          + cw[0:1] * cbuf[:, pad - 2:pad - 2 + tt, :])
    ob_ref[...] = (bg.reshape(nb, tt, W_HALF) * cx).astype(BF16)
    tail = cbuf[:, pad + tt - (K_SHORT - 1):pad + tt, :]
    cbuf[:, pad - (K_SHORT - 1):pad, :] = tail
    hout_ref[...] = tail


def _const_spec(shape):
    return pl.BlockSpec(shape, lambda *_: (0,) * len(shape), pipeline_mode=pl.Buffered(1))


def _layer_weight(w):
    stacked, layer = w
    spec = pl.BlockSpec((None,) + stacked.shape[1:], lambda *_: (layer,) + (0,) * (stacked.ndim - 1),
                        pipeline_mode=pl.Buffered(1))
    return stacked, spec


def _tiles(b, t):
    if t >= TIME_TILE:
        assert t % TIME_TILE == 0
        return 1, TIME_TILE
    assert t % (2 * SUBLANES) == 0
    return b, t


def _even_in(x, hist_b, c0, gmix, w_main, w_f, b_f, gq, gk, conv_w, hmat, prev_kv=None):
    b, t, d = x.shape
    assert t >= K_SHORT - 1
    nb, tt = _tiles(b, t)
    grid = (b // nb, t // tt)
    row = lambda shape: pl.BlockSpec(shape, lambda i, j: (i, j, 0))
    per_b = lambda shape: pl.BlockSpec(shape, lambda i, j: (i, 0, 0))
    heads = pl.BlockSpec((nb, N_HEADS, tt, LANES), lambda i, j: (i, 0, j, 0))
    kv_time_minor = tt % LANES == 0
    n_prev = prev_kv[0].shape[0] if (kv_time_minor and prev_kv is not None) else 0
    prev_specs, prev_args = [], []
    if kv_time_minor:
        stack_spec = lambda n: pl.BlockSpec((n, nb, N_HEADS, D_HEAD, tt), lambda i, j: (0, i, 0, 0, j))
        kv_shape = jax.ShapeDtypeStruct((n_prev + 1, b, N_HEADS, D_HEAD, t), F32)
        kv_spec = stack_spec(n_prev + 1)
        if n_prev:
            prev_specs, prev_args = [stack_spec(n_prev)] * 2, list(prev_kv)
    else:
        kv_shape = jax.ShapeDtypeStruct((b, t, W_HALF), F32)
        kv_spec = row((nb, tt, W_HALF))
    out_shape = (
        jax.ShapeDtypeStruct((b, N_HEADS, t, LANES), BF16),
        jax.ShapeDtypeStruct((b, N_HEADS, t, LANES), BF16),
        jax.ShapeDtypeStruct((b, N_HEADS, t, LANES), BF16),
        kv_shape,
        kv_shape,
        jax.ShapeDtypeStruct((b, t, N_HEADS), F32),
        jax.ShapeDtypeStruct((b, t, W_HALF), BF16),
        jax.ShapeDtypeStruct((b, K_SHORT - 1, W_HALF), F32),
        jax.ShapeDtypeStruct((b, t // tt, SUBLANES, LANES), F32),
    )
    return pl.pallas_call(
        functools.partial(_even_in_kernel, nb=nb, tt=tt, kv_time_minor=kv_time_minor, n_prev=n_prev),
        grid=grid,
        in_specs=[row((nb, tt, d)), per_b((nb, K_SHORT - 1, W_HALF)), per_b((nb, 1, LANES)),
                  _const_spec(gmix.shape), _const_spec(w_main.shape), _const_spec(w_f.shape),
                  _const_spec(b_f.shape), _const_spec(gq.shape), _const_spec(gk.shape),
                  _const_spec(conv_w.shape), _const_spec(hmat.shape)] + prev_specs,
        out_specs=(heads, heads, heads, kv_spec, kv_spec, row((nb, tt, N_HEADS)),
                   row((nb, tt, W_HALF)), per_b((nb, K_SHORT - 1, W_HALF)),
                   pl.BlockSpec((nb, 1, SUBLANES, LANES), lambda i, j: (i, j, 0, 0))),
        out_shape=out_shape,
        scratch_shapes=[pltpu.VMEM((nb, tt + SUBLANES, W_HALF), F32), pltpu.VMEM((nb, 1, LANES), F32)],
        compiler_params=pltpu.CompilerParams(dimension_semantics=("arbitrary", "arbitrary"),
                                             vmem_limit_bytes=VMEM_LIMIT),
        name="even_in",
    )(x, hist_b, c0, gmix, w_main, w_f, b_f, gq, gk, conv_w, hmat, *prev_args)


def _cache_prep_kernel(k_ref, v_ref, lf_ref, ka_ref, va_ref, ctot_ref, ccar, *, tt):
    @pl.when(pl.program_id(1) == 0)
    def _():
        ccar[...] = jnp.zeros_like(ccar)

    def time_major(ref):
        pairs = [ref[0, 2 * p:2 * p + 2].reshape(LANES, tt).T for p in range(N_HEADS // 2)]
        return jnp.concatenate(pairs, axis=1)

    lf = jnp.concatenate([lf_ref[0], jnp.zeros((LANES - N_HEADS, tt), F32)], axis=0).T
    c2 = _tile_cumsum(lf, ccar, 1, tt) * LOG2E
    _augment_heads(time_major(k_ref), c2, 'k', ka_ref, 1, tt)
    _augment_heads(time_major(v_ref), None, 'v', va_ref, 1, tt)
    ctot_ref[...] = ccar[...]


def _cache_prep(k, v, lf, layer):
    _, b, _, _, t = k.shape
    tt = min(t, TIME_TILE)
    assert t % tt == 0 and tt % LANES == 0
    kv_in = pl.BlockSpec((None, 1, N_HEADS, D_HEAD, tt), lambda i, j: (layer, i, 0, 0, j))
    lf_in = pl.BlockSpec((None, 1, N_HEADS, tt), lambda i, j: (layer, i, 0, j))
    heads = pl.BlockSpec((1, N_HEADS, tt, LANES), lambda i, j: (i, 0, j, 0))
    return pl.pallas_call(
        functools.partial(_cache_prep_kernel, tt=tt),
        grid=(b, t // tt),
        in_specs=[kv_in, kv_in, lf_in],
        out_specs=(heads, heads, pl.BlockSpec((1, 1, LANES), lambda i, j: (i, 0, 0))),
        out_shape=(jax.ShapeDtypeStruct((b, N_HEADS, t, LANES), BF16),
                   jax.ShapeDtypeStruct((b, N_HEADS, t, LANES), BF16),
                   jax.ShapeDtypeStruct((b, 1, LANES), F32)),
        scratch_shapes=[pltpu.VMEM((1, 1, LANES), F32)],
        compiler_params=pltpu.CompilerParams(dimension_semantics=("arbitrary", "arbitrary"),
                                             vmem_limit_bytes=VMEM_LIMIT),
        name="cache_prep",
    )(k, v, lf)


def _attn_kernel(thr_ref, klast_ref, qa_ref, kf_ref, vf_ref, *rest, tq, tk, past, skip, split, wide):
    o_ref = rest[-1]
    i = pl.program_id(2)
    diag_off = past + i * tq
    n_full = diag_off // tk
    kd_ref, vd_ref, doff = (rest[0], rest[1], 0) if split else (kf_ref, vf_ref, pl.multiple_of(diag_off, tq))
    nt = (((1,), (1,)), ((), ()))
    qs = [qa_ref[0, hh] for hh in range(2)]
    heads = range(2)

    def scores(hh, k_ref, off, size):
        return lax.dot_general(qs[hh], k_ref[0, hh, pl.ds(off, size), :], nt, preferred_element_type=F32)

    def consume(state, hh, s, v_ref, off, size):
        m, acc = state
        m_new = jnp.maximum(m, jnp.max(s, axis=-1, keepdims=True))
        p = jnp.exp2(s - m_new).astype(BF16)
        pv = jnp.dot(p, v_ref[0, hh, pl.ds(off, size), :], preferred_element_type=F32)
        return m_new, jnp.exp2(m - m_new) * acc + pv

    init = tuple((jnp.full((tq, 1), NEG_BIG, F32), jnp.zeros((tq, LANES), F32)) for _ in heads)

    def body(j, states):
        off = pl.multiple_of(j * tk, tk)
        return tuple(consume(states[hh], hh, scores(hh, kf_ref, off, tk), vf_ref, off, tk) for hh in heads)

    first = 0
    if skip:
        def skippable(hh):
            r = (pl.program_id(0) * pl.num_programs(1) + pl.program_id(1)) * 2 + hh
            thr = thr_ref[r, i]
            return lax.fori_loop(0, n_full, lambda j, n: n + (klast_ref[r, j] >= thr).astype(jnp.int32), 0)

        first = jnp.minimum(skippable(0), skippable(1))
    if wide > 1:
        def wide_body(j, states):
            off = pl.multiple_of((first + wide * j) * tk, tk)
            return tuple(consume(states[hh], hh, scores(hh, kf_ref, off, wide * tk), vf_ref, off, wide * tk)
                         for hh in heads)

        n_wide = (n_full - first) // wide
        init = lax.fori_loop(0, n_wide, wide_body, init)
        first = first + wide * n_wide
    states = lax.fori_loop(first, n_full, body, init)

    mask = lax.broadcasted_iota(jnp.int32, (tq, tq), 0) >= lax.broadcasted_iota(jnp.int32, (tq, tq), 1)
    outs = []
    for hh in heads:
        s = jnp.where(mask, scores(hh, kd_ref, doff, tq), NEG_BIG)
        _, acc = consume(states[hh], hh, s, vd_ref, doff, tq)
        outs.append(acc / acc[:, D_HEAD:D_HEAD + 1])
    lane = lax.broadcasted_iota(jnp.int32, (1, LANES), 1)
    o_ref[0] = jnp.where(lane < D_HEAD, outs[0], pltpu.roll(outs[1], D_HEAD, axis=1)).astype(BF16)


def _skip_tables(edges, tile, block, g_q, g_k):
    b = edges.shape[0]
    per = block // tile
    qk_bound = 1.02 * D_HEAD ** 0.5 * LOG2E * jnp.max(jnp.abs(g_q)) * jnp.max(jnp.abs(g_k))
    margin = 2.0 * qk_bound + (EXP2_UNDERFLOW + 1.0)
    q_first = edges[:, 0::per, 0, :N_HEADS]
    k_last = edges[:, per - 1::per, 1, :N_HEADS]
    to_rows = lambda a: jnp.transpose(a, (0, 2, 1)).reshape(b * N_HEADS, -1)
    return to_rows(q_first + margin), to_rows(k_last)


def _attention(qa, ka, va, skip_tables=None, cache=None):
    b, h, t_q, _ = qa.shape
    split = cache is not None
    kf, vf = cache if split else (ka, va)
    t_f = kf.shape[2]
    past = t_f if split else 0
    tq = min(ATT_BLOCK, t_q)
    tk = math.gcd(past, ATT_CACHE_BLOCK) if split else tq
    assert t_q % tq == 0 and past % tk == 0 and (tq % tk == 0 or t_q == tq)
    skip = skip_tables is not None
    if skip:
        assert not split and h == N_HEADS
        thr, k_last = skip_tables
    else:
        thr = k_last = jnp.zeros((1, 1), F32)
    kv_bytes = 2 * t_f * LANES * 2
    kv_mode = dict(pipeline_mode=pl.Buffered(1)) if 4 * kv_bytes > VMEM_LIMIT // 4 else {}
    kv_spec = pl.BlockSpec((1, 2, t_f, LANES), lambda i, p, j: (i, p, 0, 0), **kv_mode)
    q_spec = pl.BlockSpec((1, 2, tq, LANES), lambda i, p, j: (i, p, j, 0))
    smem = pl.BlockSpec(memory_space=pltpu.SMEM)
    own_specs, own_args = ([q_spec, q_spec], [ka, va]) if split else ([], [])
    return pl.pallas_call(
        functools.partial(_attn_kernel, tq=tq, tk=tk, past=past, skip=skip, split=split,
                          wide=1 if split else ATT_WIDE),
        grid=(b, h // 2, t_q // tq),
        in_specs=[smem, smem, q_spec, kv_spec, kv_spec] + own_specs,
        out_specs=pl.BlockSpec((1, tq, LANES), lambda i, p, j: (i, j, p)),
        out_shape=jax.ShapeDtypeStruct((b, t_q, h * D_HEAD), BF16),
        compiler_params=pltpu.CompilerParams(dimension_semantics=("arbitrary", "arbitrary", "arbitrary"),
                                             vmem_limit_bytes=VMEM_LIMIT),
        name="fox_attention",
    )(thr, k_last, qa, kf, vf, *own_args)


def _gelu_tanh(x):
    return 0.5 * x * (1.0 + jnp.tanh(math.sqrt(2.0 / math.pi) * (x + 0.044715 * (x * x * x))))


def _long_conv(dbuf, phase, cw_ref, cd_ref, nb, tt, first):
    rows_kept = tt + first + K_LONG - 1 - SUBLANES
    for r in range(1, SUBLANES):
        phase[r - 1, :, :rows_kept, :] = dbuf[:, r:r + rows_kept, :]
    rb = CONV_ACC_ROWS // nb
    for cb in range(W_HALF // LANES):
        cols = slice(cb * LANES, (cb + 1) * LANES)
        w_cols = cw_ref[:, cols]
        for r0 in range(0, tt, rb):
            acc = None
            for k in range(K_LONG):
                r, a = (first + k) % SUBLANES, (first + k) // SUBLANES * SUBLANES
                src = dbuf if r == 0 else phase.at[r - 1]
                term = w_cols[k:k + 1] * src[:, a + r0:a + r0 + rb, cols]
                acc = term if acc is None else acc + term
            cd_ref[:, r0:r0 + rb, cols] = acc


def _odd_in_kernel(x_ref, hist_ref, gmix_ref, w_ref, gvc_ref, ws_ref, bs_ref, cw_ref, gd_ref,
                   oc_ref, od_ref, hout_ref, *rest, nb, tt, cs, emit_vc):
    vc_ref = rest[0] if emit_vc else None
    dbuf, phase, cdbuf = rest[-3:]
    tm = nb * tt
    pad = 4 * SUBLANES
    nh = K_LONG - 1

    @pl.when(pl.program_id(1) == 0)
    def _():
        dbuf[:, pad - nh:pad, :] = hist_ref[...]

    x = x_ref[...].reshape(tm, x_ref.shape[-1])
    xn = _rms(x, gmix_ref[...]).astype(BF16)

    def proj(i):
        return jnp.dot(xn, w_ref[:, i * W_HALF:(i + 1) * W_HALF], preferred_element_type=F32)

    u = _gelu_tanh(proj(0))
    vc = _rms(_gelu_tanh(proj(1)), gvc_ref[...])
    if emit_vc:
        vc_ref[...] = vc.reshape(nb, tt, W_HALF)
    vcb = vc.astype(BF16)

    row = lax.broadcasted_iota(jnp.int32, (cs, cs), 0)
    col = lax.broadcasted_iota(jnp.int32, (cs, cs), 1)
    per_b = tt // cs
    for g in range(N_GROUPS):
        wsg = jnp.where(row >= col, ws_ref[g], 0.0).astype(BF16)
        for ci in range(tm // cs):
            r0 = ci * cs
            gate = jnp.dot(wsg, vcb[r0:r0 + cs, g * LANES:(g + 1) * LANES], preferred_element_type=F32) + bs_ref[g]
            oc = u[r0:r0 + cs, g * LANES:(g + 1) * LANES] * gate
            t0 = (ci % per_b) * cs
            oc_ref[ci // per_b, t0:t0 + cs, g * LANES:(g + 1) * LANES] = oc.astype(BF16)

    glu = proj(2) * jax.nn.sigmoid(proj(3))
    dbuf[:, pad:pad + tt, :] = glu.reshape(nb, tt, W_HALF)
    _long_conv(dbuf, phase, cw_ref, cdbuf, nb, tt, pad - nh)
    y = _rms(cdbuf[...].reshape(tm, W_HALF), gd_ref[...])
    od_ref[...] = (y * jax.nn.sigmoid(y)).astype(BF16).reshape(nb, tt, W_HALF)
    tail = dbuf[:, pad + tt - nh:pad + tt, :]
    dbuf[:, pad - nh:pad, :] = tail
    hout_ref[...] = tail


def _odd_in(x, hist_d, gmix, w_in, gvc, ws, bs, conv_w, gd, emit_vc):
    b, t, d = x.shape
    assert t >= K_LONG - 1
    nb, tt = _tiles(b, t)
    cs = min(GMLP_CHUNK, tt)
    assert tt % cs == 0 and CONV_ACC_ROWS % nb == 0 and tt % (CONV_ACC_ROWS // nb) == 0
    grid = (b // nb, t // tt)
    row = lambda shape: pl.BlockSpec(shape, lambda i, j: (i, j, 0))
    per_b = lambda shape: pl.BlockSpec(shape, lambda i, j: (i, 0, 0))
    ws = ws[:, :cs, :cs]
    bs = bs[:, :cs, :]
    w_in, w_in_spec = _layer_weight(w_in)
    out_specs = [row((nb, tt, W_HALF)), row((nb, tt, W_HALF)), per_b((nb, K_LONG - 1, W_HALF))]
    out_shape = [jax.ShapeDtypeStruct((b, t, W_HALF), BF16), jax.ShapeDtypeStruct((b, t, W_HALF), BF16),
                 jax.ShapeDtypeStruct((b, K_LONG - 1, W_HALF), F32)]
    if emit_vc:
        out_specs.append(row((nb, tt, W_HALF)))
        out_shape.append(jax.ShapeDtypeStruct((b, t, W_HALF), F32))
    hist_rows = 4 * SUBLANES
    return pl.pallas_call(
        functools.partial(_odd_in_kernel, nb=nb, tt=tt, cs=cs, emit_vc=emit_vc),
        grid=grid,
        in_specs=[row((nb, tt, d)), per_b((nb, K_LONG - 1, W_HALF)),
                  _const_spec(gmix.shape), w_in_spec, _const_spec(gvc.shape),
                  _const_spec(ws.shape), _const_spec(bs.shape), _const_spec(conv_w.shape), _const_spec(gd.shape)],
        out_specs=tuple(out_specs),
        out_shape=tuple(out_shape),
        scratch_shapes=[pltpu.VMEM((nb, tt + hist_rows, W_HALF), F32),
                        pltpu.VMEM((SUBLANES - 1, nb, tt + hist_rows - SUBLANES, W_HALF), F32),
                        pltpu.VMEM((nb, tt, W_HALF), F32)],
        compiler_params=pltpu.CompilerParams(dimension_semantics=("arbitrary", "arbitrary"),
                                             vmem_limit_bytes=VMEM_LIMIT),
        name="odd_in",
    )(x, hist_d, gmix, w_in, gvc, ws, bs, conv_w, gd)


def _mix_ffn_kernel(x_ref, oa_ref, ob_ref, hist_ref, wo_ref, g_ref, wu_ref, cw_ref, wd_ref,
                    y_ref, hout_ref, hbuf, abuf, *, nb, tt):
    tm = nb * tt
    pad = SUBLANES
    d_ff = wd_ref.shape[0]

    @pl.when(pl.program_id(1) == 0)
    def _():
        hbuf[:, pad - (K_SHORT - 1):pad, :] = hist_ref[...]

    d = x_ref.shape[-1]
    oa = oa_ref[...].reshape(tm, W_HALF)
    ob = ob_ref[...].reshape(tm, W_HALF)
    x1 = (x_ref[...].reshape(tm, d)
          + jnp.dot(oa, wo_ref[:W_HALF, :], preferred_element_type=F32)
          + jnp.dot(ob, wo_ref[W_HALF:, :], preferred_element_type=F32))
    xn = _rms(x1, g_ref[...]).astype(BF16)

    def conv_up(lo):
        h = jnp.dot(xn, wu_ref[:, lo:lo + d_ff], preferred_element_type=F32)
        hbuf[:, pad:pad + tt, lo:lo + d_ff] = h.reshape(nb, tt, d_ff)
        cw = cw_ref[:, lo:lo + d_ff]
        return (cw[2:3] * hbuf[:, pad:pad + tt, lo:lo + d_ff]
                + cw[1:2] * hbuf[:, pad - 1:pad - 1 + tt, lo:lo + d_ff]
                + cw[0:1] * hbuf[:, pad - 2:pad - 2 + tt, lo:lo + d_ff]).reshape(tm, d_ff)

    gate = conv_up(0)
    val = conv_up(d_ff)
    abuf[...] = (gate * jax.nn.sigmoid(gate) * val).astype(BF16)
    y_ref[...] = (x1 + jnp.dot(abuf[...], wd_ref[...], preferred_element_type=F32)).reshape(nb, tt, d)
    tail = hbuf[:, pad + tt - (K_SHORT - 1):pad + tt, :]
    hbuf[:, pad - (K_SHORT - 1):pad, :] = tail
    hout_ref[...] = tail


def _mix_ffn(x, oa, ob, hist_ffn, w_out, g_ffn, w_up, conv_w, w_down):
    b, t, d = x.shape
    assert t >= K_SHORT - 1
    nb, tt = _tiles(b, t)
    grid = (b // nb, t // tt)
    (w_out, w_out_spec), (w_up, w_up_spec), (w_down, w_down_spec) = map(_layer_weight, (w_out, w_up, w_down))
    two_ff = w_up.shape[-1]
    row = lambda shape: pl.BlockSpec(shape, lambda i, j: (i, j, 0))
    per_b = lambda shape: pl.BlockSpec(shape, lambda i, j: (i, 0, 0))
    return pl.pallas_call(
        functools.partial(_mix_ffn_kernel, nb=nb, tt=tt),
        grid=grid,
        in_specs=[row((nb, tt, d)), row((nb, tt, W_HALF)), row((nb, tt, W_HALF)), per_b((nb, K_SHORT - 1, two_ff)),
                  w_out_spec, _const_spec(g_ffn.shape), w_up_spec, _const_spec(conv_w.shape), w_down_spec],
        out_specs=(row((nb, tt, d)), per_b((nb, K_SHORT - 1, two_ff))),
        out_shape=(jax.ShapeDtypeStruct((b, t, d), F32),
                   jax.ShapeDtypeStruct((b, K_SHORT - 1, two_ff), F32)),
        scratch_shapes=[pltpu.VMEM((nb, tt + SUBLANES, two_ff), F32), pltpu.VMEM((nb * tt, two_ff // 2), BF16)],
        compiler_params=pltpu.CompilerParams(dimension_semantics=("arbitrary", "arbitrary"),
                                             vmem_limit_bytes=VMEM_LIMIT),
        name="mix_ffn",
    )(x, oa, ob, hist_ffn, w_out, g_ffn, w_up, conv_w, w_down)


def _prep_params(g_mix, w_in_even, b_f, g_q, g_k, conv_b, w_out_even, w_in_odd, g_vc, w_s, b_s,
                 conv_d, g_d, w_out_odd, g_ffn, w_up, conv_ffn, w_down):
    depth = g_mix.shape[0]
    off_f = 3 * W_HALF
    head_id = jnp.arange(W_HALF) // D_HEAD
    hmat = jnp.where(head_id[:, None] == head_id[None, :], 1.0 / D_HEAD, 0.0).astype(BF16)
    w_up, w_down, w_in_odd, w_out_even, w_out_odd = (
        w.astype(BF16) for w in (w_up, w_down, w_in_odd, w_out_even, w_out_odd))
    layers = []
    for l in range(depth):
        i = l // 2
        p = dict(g_mix=g_mix[l][None], g_ffn=g_ffn[l][None], w_up=(w_up, l),
                 conv_ffn=conv_ffn[l], w_down=(w_down, l))
        if l % 2 == 0:
            w = w_in_even[i]
            p.update(
                w_main=jnp.concatenate([w[:, :off_f], w[:, off_f + N_HEADS:]], axis=1).astype(BF16),
                w_f=jnp.pad(w[:, off_f:off_f + N_HEADS], ((0, 0), (0, LANES - N_HEADS))).astype(BF16),
                b_f=jnp.pad(b_f[i], (0, LANES - N_HEADS))[None],
                g_q=jnp.tile(g_q[i], N_HEADS)[None], g_k=jnp.tile(g_k[i], N_HEADS)[None],
                conv_w=conv_b[i], w_out=(w_out_even, i), hmat=hmat)
        else:
            p.update(
                w_in=(w_in_odd, i), g_vc=g_vc[i][None], w_s=w_s[i],
                b_s=jnp.broadcast_to(b_s[i][:, :, None], b_s[i].shape + (LANES,)),
                conv_w=conv_d[i], g_d=g_d[i][None], w_out=(w_out_odd, i))
        layers.append(p)
    return layers


def _trunk(x, layers, hist_b, hist_d, hist_ffn, att_cache):
    b = x.shape[0]
    ks, vs, lfs, hbs, vcs, hds, hfs = [], [], [], [], [], [], []
    kv_stack = None
    if att_cache is not None:
        cache_k = jnp.transpose(att_cache[0], (0, 1, 3, 4, 2))
        cache_v = jnp.transpose(att_cache[1], (0, 1, 3, 4, 2))
        cache_lf = jnp.transpose(att_cache[2], (0, 1, 3, 2))
    for l, p in enumerate(layers):
        i = l // 2
        if l % 2 == 0:
            if att_cache is None:
                c0 = jnp.zeros((b, 1, LANES), F32)
            else:
                ka_c, va_c, c0 = _cache_prep(cache_k, cache_v, cache_lf, i)
            qa, ka, va, k, v, lf, ob, hb, edges = _even_in(x, hist_b[i], c0, p['g_mix'], p['w_main'], p['w_f'],
                                                           p['b_f'], p['g_q'], p['g_k'], p['conv_w'], p['hmat'],
                                                           prev_kv=kv_stack)
            t = x.shape[1]
            if att_cache is not None:
                oa = _attention(qa, ka, va, cache=(ka_c, va_c))
            else:
                tables = _skip_tables(edges, t // edges.shape[1], min(ATT_BLOCK, t), p['g_q'], p['g_k'])
                oa = _attention(qa, ka, va, tables)
            if k.ndim == 5:
                kv_stack = (k, v)
            else:
                ks.append(k.reshape(b, t, N_HEADS, D_HEAD))
                vs.append(v.reshape(b, t, N_HEADS, D_HEAD))
            lfs.append(lf)
            hbs.append(hb)
        else:
            oa, ob, hd, *vc = _odd_in(x, hist_d[i], p['g_mix'], p['w_in'], p['g_vc'], p['w_s'], p['b_s'],
                                      p['conv_w'], p['g_d'], emit_vc=att_cache is not None)
            vcs.extend(vc)
            hds.append(hd)
        x, hf = _mix_ffn(x, oa, ob, hist_ffn[l], p['w_out'], p['g_ffn'], p['w_up'], p['conv_ffn'], p['w_down'])
        hfs.append(hf)
    if kv_stack is not None:
        all_k, all_v = (jnp.transpose(a, (0, 1, 4, 2, 3)) for a in kv_stack)
    else:
        all_k, all_v = jnp.stack(ks), jnp.stack(vs)
    return (x, all_k, all_v, jnp.stack(lfs), jnp.stack(hbs),
            jnp.stack(vcs) if vcs else None, jnp.stack(hds), jnp.stack(hfs))


def kernel(x_prompt, x_sample, cache_k, cache_v, cache_logf, state_conv_b, state_conv_d, state_conv_ffn,
           g_mix, w_in_even, b_f, g_q, g_k, conv_b, w_out_even, w_in_odd, g_vc, w_s, b_s,
           conv_d, g_d, w_out_odd, g_ffn, w_up, conv_ffn, w_down):
    layers = _prep_params(g_mix, w_in_even, b_f, g_q, g_k, conv_b, w_out_even, w_in_odd, g_vc, w_s, b_s,
                          conv_d, g_d, w_out_odd, g_ffn, w_up, conv_ffn, w_down)
    b = x_prompt.shape[0]
    depth = g_mix.shape[0]
    n_even, n_odd = (depth + 1) // 2, depth // 2
    zb = jnp.zeros((n_even, b, K_SHORT - 1, W_HALF), F32)
    zd = jnp.zeros((n_odd, b, K_LONG - 1, W_HALF), F32)
    zf = jnp.zeros((depth, b, K_SHORT - 1, w_up.shape[-1]), F32)
    (y_prompt, p_k, p_v, p_logf, p_conv_b, _, p_conv_d, p_conv_ffn) = _trunk(
        x_prompt, layers, zb, zd, zf, None)
    (y_sample, s_k, s_v, s_logf, s_conv_b, s_vc, s_conv_d, s_conv_ffn) = _trunk(
        x_sample, layers, state_conv_b, state_conv_d, state_conv_ffn, (cache_k, cache_v, cache_logf))
    return (y_prompt, y_sample, p_k, p_v, p_logf, p_conv_b, p_conv_d, p_conv_ffn,
            s_k, s_v, s_logf, s_conv_b, s_vc, s_conv_d, s_conv_ffn)
```

```python
import functools
import math

import jax
import jax.numpy as jnp
from jax import lax
from jax.experimental import pallas as pl
from jax.experimental.pallas import tpu as pltpu

F32 = jnp.float32
BF16 = jnp.bfloat16

EPS = 1e-6
LOG2E = math.log2(math.e)
LANES = 128
SUBLANES = 8
N_HEADS = 8
D_HEAD = 64
W_HALF = 512
K_SHORT = 3
K_LONG = 31
GMLP_CHUNK = 128
N_GROUPS = 4
TIME_TILE = 512
ATT_BLOCK = 1024
ATT_WIDE = 2
ATT_CACHE_BLOCK = 4096
NEG_BIG = -1e30
EXP2_UNDERFLOW = 150.0
CONV_ACC_ROWS = 128
VMEM_LIMIT = 56 * 1024 * 1024


def _rms(x, g):
    return x * lax.rsqrt(jnp.mean(x * x, axis=-1, keepdims=True) + EPS) * g


def _split3(x):
    hi = x.astype(BF16).astype(F32)
    r = x - hi
    mid = r.astype(BF16).astype(F32)
    lo = r - mid
    return hi, mid, lo


def _tile_cumsum(lf, carry_ref, nb, tt):
    row = lax.broadcasted_iota(jnp.int32, (tt, tt), 0)
    col = lax.broadcasted_iota(jnp.int32, (tt, tt), 1)
    tri = jnp.where(row >= col, 1.0, 0.0).astype(BF16)
    hi, mid, lo = _split3(lf)
    cat = jnp.concatenate([hi, mid, lo], axis=1).astype(BF16)
    outs = []
    for b in range(nb):
        r = jnp.dot(tri, cat[b * tt:(b + 1) * tt], preferred_element_type=F32)
        c_b = r[:, :LANES] + r[:, LANES:2 * LANES] + r[:, 2 * LANES:] + carry_ref[b]
        carry_ref[b] = c_b[tt - 1:tt, :]
        outs.append(c_b)
    return outs[0] if nb == 1 else jnp.concatenate(outs, axis=0)


def _lane_consts():
    lane = lax.broadcasted_iota(jnp.int32, (1, LANES), 1)
    e = lambda i: jnp.where(lane == i, 1.0, 0.0).astype(F32)
    ones_a = e(D_HEAD) + e(D_HEAD + 1) + e(D_HEAD + 2)
    ones_b = e(D_HEAD + 3) + e(D_HEAD + 4) + e(D_HEAD + 5)
    return lane < D_HEAD, e, ones_a, ones_b


def _augment_heads(a, c2, kind, out_ref, nb, tt, split_once=False, spread=None):
    head_lanes, e, ones_a, ones_b = _lane_consts()
    tm = nb * tt
    bcast = lambda x, h: jnp.broadcast_to(x[:, h:h + 1], (tm, LANES))
    pieces = _split3(c2) if split_once and kind != 'v' else None
    spread_ext = None
    if spread is not None:
        assert kind == 'k'
        packed = jnp.concatenate(_split3(c2), axis=1).astype(BF16)
        spread_ext = jnp.dot(packed, spread, preferred_element_type=F32)
    for h in range(N_HEADS):
        pair = a[:, LANES * (h // 2):LANES * (h // 2 + 1)]
        if h % 2:
            pair = pltpu.roll(pair, D_HEAD, axis=1)
        if kind == 'v':
            ext = e(D_HEAD)
        elif spread_ext is not None:
            ext = ones_a + spread_ext[:, h * LANES:(h + 1) * LANES]
        else:
            hi, mid, lo = [bcast(x, h) for x in pieces] if pieces else _split3(bcast(c2, h))
            if kind == 'q':
                ext = hi * e(D_HEAD) + mid * e(D_HEAD + 1) + lo * e(D_HEAD + 2) + ones_b
            else:
                ext = ones_a - (hi * e(D_HEAD + 3) + mid * e(D_HEAD + 4) + lo * e(D_HEAD + 5))
        out_ref[:, h] = jnp.where(head_lanes, pair, ext).astype(BF16).reshape(nb, tt, LANES)


def _log_sigmoid(z):
    return -(jnp.maximum(-z, 0.0) + jnp.log1p(jnp.exp(-jnp.abs(z))))


def _masked_logf(z):
    lane = lax.broadcasted_iota(jnp.int32, (1, LANES), 1)
    return jnp.where(lane < N_HEADS, _log_sigmoid(z), 0.0)


def _even_in_kernel(x_ref, hist_ref, c0_ref, gmix_ref, w_ref, wf_ref, bf_ref, gq_ref, gk_ref, cw_ref, hm_ref,
                    *rest, nb, tt, kv_time_minor, n_prev):
    kprev_ref, vprev_ref = rest[:2] if n_prev else (None, None)
    (qa_ref, ka_ref, va_ref, k_ref, v_ref, lf_ref, ob_ref, hout_ref, edge_ref, cbuf, ccar) = rest[2 if n_prev else 0:]
    tm = nb * tt
    pad = SUBLANES

    @pl.when(pl.program_id(1) == 0)
    def _():
        cbuf[:, pad - (K_SHORT - 1):pad, :] = hist_ref[...]
        ccar[...] = c0_ref[...]

    x = x_ref[...].reshape(tm, x_ref.shape[-1])
    xn = _rms(x, gmix_ref[...]).astype(BF16)

    def proj(i):
        return jnp.dot(xn, w_ref[:, i * W_HALF:(i + 1) * W_HALF], preferred_element_type=F32)

    hm = hm_ref[...]

    def headnorm(a, g):
        ms = jnp.dot((a * a).astype(BF16), hm, preferred_element_type=F32)
        return a * lax.rsqrt(ms + EPS) * g

    qn = headnorm(proj(0), gq_ref[...])
    kn = headnorm(proj(1), gk_ref[...])
    v = proj(2)
    if kv_time_minor:
        if n_prev:
            k_ref[:n_prev] = kprev_ref[...]
            v_ref[:n_prev] = vprev_ref[...]
        for b in range(nb):
            k_ref[n_prev, b] = kn[b * tt:(b + 1) * tt, :].T.reshape(N_HEADS, D_HEAD, tt)
            v_ref[n_prev, b] = v[b * tt:(b + 1) * tt, :].T.reshape(N_HEADS, D_HEAD, tt)
    else:
        k_ref[...] = kn.reshape(nb, tt, W_HALF)
        v_ref[...] = v.reshape(nb, tt, W_HALF)

    z = jnp.dot(xn, wf_ref[...], preferred_element_type=F32) + bf_ref[...]
    lf = _masked_logf(z)
    lf_ref[...] = lf[:, :N_HEADS].reshape(nb, tt, N_HEADS)
    c2 = _tile_cumsum(lf, ccar, nb, tt) * LOG2E
    for b in range(nb):
        edge_ref[b, 0, 0:1, :] = c2[b * tt:b * tt + 1, :]
        edge_ref[b, 0, 1:2, :] = c2[(b + 1) * tt - 1:(b + 1) * tt, :]
        edge_ref[b, 0, 2:, :] = jnp.zeros((SUBLANES - 2, LANES), F32)

    _augment_heads(qn * (LOG2E * D_HEAD ** -0.5), c2, 'q', qa_ref, nb, tt, split_once=True)
    _augment_heads(kn, c2, 'k', ka_ref, nb, tt, split_once=True)
    _augment_heads(v, None, 'v', va_ref, nb, tt)

    bg = proj(3)
    cgx = proj(4) * proj(5)
    cbuf[:, pad:pad + tt, :] = cgx.reshape(nb, tt, W_HALF)
    cw = cw_ref[...]
    cx = (cw[2:3] * cbuf[:, pad:pad + tt, :] + cw[1:2] * cbuf[:, pad - 1:pad - 1 + tt, :]
          + cw[0:1] * cbuf[:, pad - 2:pad - 2 + tt, :])
    ob_ref[...] = (bg.reshape(nb, tt, W_HALF) * cx).astype(BF16)
    tail = cbuf[:, pad + tt - (K_SHORT - 1):pad + tt, :]
    cbuf[:, pad - (K_SHORT - 1):pad, :] = tail
    hout_ref[...] = tail


def _const_spec(shape):
    return pl.BlockSpec(shape, lambda *_: (0,) * len(shape), pipeline_mode=pl.Buffered(1))


def _layer_weight(w):
    stacked, layer = w
    spec = pl.BlockSpec((None,) + stacked.shape[1:], lambda *_: (layer,) + (0,) * (stacked.ndim - 1),
                        pipeline_mode=pl.Buffered(1))
    return stacked, spec


def _tiles(b, t):
    if t >= TIME_TILE:
        assert t % TIME_TILE == 0
        return 1, TIME_TILE
    assert t % (2 * SUBLANES) == 0
    return b, t


def _even_in(x, hist_b, c0, gmix, w_main, w_f, b_f, gq, gk, conv_w, hmat, prev_kv=None):
    b, t, d = x.shape
    assert t >= K_SHORT - 1
    nb, tt = _tiles(b, t)
    grid = (b // nb, t // tt)
    row = lambda shape: pl.BlockSpec(shape, lambda i, j: (i, j, 0))
    per_b = lambda shape: pl.BlockSpec(shape, lambda i, j: (i, 0, 0))
    heads = pl.BlockSpec((nb, N_HEADS, tt, LANES), lambda i, j: (i, 0, j, 0))
    kv_time_minor = tt % LANES == 0
    n_prev = prev_kv[0].shape[0] if (kv_time_minor and prev_kv is not None) else 0
    prev_specs, prev_args = [], []
    if kv_time_minor:
        stack_spec = lambda n: pl.BlockSpec((n, nb, N_HEADS, D_HEAD, tt), lambda i, j: (0, i, 0, 0, j))
        kv_shape = jax.ShapeDtypeStruct((n_prev + 1, b, N_HEADS, D_HEAD, t), F32)
        kv_spec = stack_spec(n_prev + 1)
        if n_prev:
            prev_specs, prev_args = [stack_spec(n_prev)] * 2, list(prev_kv)
    else:
        kv_shape = jax.ShapeDtypeStruct((b, t, W_HALF), F32)
        kv_spec = row((nb, tt, W_HALF))
    out_shape = (
        jax.ShapeDtypeStruct((b, N_HEADS, t, LANES), BF16),
        jax.ShapeDtypeStruct((b, N_HEADS, t, LANES), BF16),
        jax.ShapeDtypeStruct((b, N_HEADS, t, LANES), BF16),
        kv_shape,
        kv_shape,
        jax.ShapeDtypeStruct((b, t, N_HEADS), F32),
        jax.ShapeDtypeStruct((b, t, W_HALF), BF16),
        jax.ShapeDtypeStruct((b, K_SHORT - 1, W_HALF), F32),
        jax.ShapeDtypeStruct((b, t // tt, SUBLANES, LANES), F32),
    )
    return pl.pallas_call(
        functools.partial(_even_in_kernel, nb=nb, tt=tt, kv_time_minor=kv_time_minor, n_prev=n_prev),
        grid=grid,
        in_specs=[row((nb, tt, d)), per_b((nb, K_SHORT - 1, W_HALF)), per_b((nb, 1, LANES)),
                  _const_spec(gmix.shape), _const_spec(w_main.shape), _const_spec(w_f.shape),
                  _const_spec(b_f.shape), _const_spec(gq.shape), _const_spec(gk.shape),
                  _const_spec(conv_w.shape), _const_spec(hmat.shape)] + prev_specs,
        out_specs=(heads, heads, heads, kv_spec, kv_spec, row((nb, tt, N_HEADS)),
                   row((nb, tt, W_HALF)), per_b((nb, K_SHORT - 1, W_HALF)),
                   pl.BlockSpec((nb, 1, SUBLANES, LANES), lambda i, j: (i, j, 0, 0))),
        out_shape=out_shape,
        scratch_shapes=[pltpu.VMEM((nb, tt + SUBLANES, W_HALF), F32), pltpu.VMEM((nb, 1, LANES), F32)],
        compiler_params=pltpu.CompilerParams(dimension_semantics=("arbitrary", "arbitrary"),
                                             vmem_limit_bytes=VMEM_LIMIT),
        name="even_in",
    )(x, hist_b, c0, gmix, w_main, w_f, b_f, gq, gk, conv_w, hmat, *prev_args)


def _cache_prep_kernel(k_ref, lf_ref, spread_ref, ka_ref, ctot_ref, ccar, *, tt):
    @pl.when(pl.program_id(1) == 0)
    def _():
        ccar[...] = jnp.zeros_like(ccar)

    pairs = [k_ref[0, 2 * p:2 * p + 2].reshape(LANES, tt).T for p in range(N_HEADS // 2)]
    lf = jnp.concatenate([lf_ref[0], jnp.zeros((LANES - N_HEADS, tt), F32)], axis=0).T
    c2 = _tile_cumsum(lf, ccar, 1, tt) * LOG2E
    _augment_heads(jnp.concatenate(pairs, axis=1), c2, 'k', ka_ref, 1, tt, spread=spread_ref[...])
    ctot_ref[...] = ccar[...]


def _cache_prep(k, lf, layer):
    _, b, _, _, t = k.shape
    tt = min(t, TIME_TILE)
    assert t % tt == 0 and tt % LANES == 0
    k_in = pl.BlockSpec((None, 1, N_HEADS, D_HEAD, tt), lambda i, j: (layer, i, 0, 0, j))
    lf_in = pl.BlockSpec((None, 1, N_HEADS, tt), lambda i, j: (layer, i, 0, j))
    heads = pl.BlockSpec((1, N_HEADS, tt, LANES), lambda i, j: (i, 0, j, 0))
    rows = jnp.arange(3 * LANES)[:, None]
    cols = jnp.arange(N_HEADS * LANES)[None, :]
    spread = jnp.where((rows % LANES == cols // LANES) & (cols % LANES == D_HEAD + 3 + rows // LANES),
                       -1.0, 0.0).astype(BF16)
    return pl.pallas_call(
        functools.partial(_cache_prep_kernel, tt=tt),
        grid=(b, t // tt),
        in_specs=[k_in, lf_in, _const_spec(spread.shape)],
        out_specs=(heads, pl.BlockSpec((1, 1, LANES), lambda i, j: (i, 0, 0))),
        out_shape=(jax.ShapeDtypeStruct((b, N_HEADS, t, LANES), BF16),
                   jax.ShapeDtypeStruct((b, 1, LANES), F32)),
        scratch_shapes=[pltpu.VMEM((1, 1, LANES), F32)],
        compiler_params=pltpu.CompilerParams(dimension_semantics=("arbitrary", "arbitrary"),
                                             vmem_limit_bytes=VMEM_LIMIT),
        name="cache_prep",
    )(k, lf, spread)


def _attn_kernel(thr_ref, klast_ref, qa_ref, kf_ref, vf_ref, *rest, tq, tk, past, skip, split, wide):
    o_ref = rest[-1]
    i = pl.program_id(2)
    diag_off = past + i * tq
    n_full = diag_off // tk
    kd_ref, vd_ref, doff = (rest[0], rest[1], 0) if split else (kf_ref, vf_ref, pl.multiple_of(diag_off, tq))
    nt = (((1,), (1,)), ((), ()))
    qs = [qa_ref[0, hh] for hh in range(2)]
    heads = range(2)

    def scores(hh, k_ref, off, size):
        return lax.dot_general(qs[hh], k_ref[0, hh, pl.ds(off, size), :], nt, preferred_element_type=F32)

    def consume(state, hh, s, v_ref, off, size):
        m, acc = state
        m_new = jnp.maximum(m, jnp.max(s, axis=-1, keepdims=True))
        p = jnp.exp2(s - m_new).astype(BF16)
        pv = jnp.dot(p, v_ref[0, hh, pl.ds(off, size), :], preferred_element_type=F32)
        return m_new, jnp.exp2(m - m_new) * acc + pv

    init = tuple((jnp.full((tq, 1), NEG_BIG, F32), jnp.zeros((tq, LANES), F32)) for _ in heads)

    def body(j, states):
        off = pl.multiple_of(j * tk, tk)
        return tuple(consume(states[hh], hh, scores(hh, kf_ref, off, tk), vf_ref, off, tk) for hh in heads)

    def cached_step(hh):
        vt = vf_ref[0, hh].astype(BF16)
        one_row = jnp.where(lax.broadcasted_iota(jnp.int32, (D_HEAD, tk), 0) == 0, 1.0, 0.0).astype(BF16)
        vt_aug = jnp.concatenate([vt, one_row], axis=0)
        s = scores(hh, kf_ref, 0, tk)
        m = jnp.max(s, axis=-1, keepdims=True)
        p = jnp.exp2(s - m).astype(BF16)
        return m, lax.dot_general(p, vt_aug, nt, preferred_element_type=F32)

    first = 0
    if skip:
        def skippable(hh):
            r = (pl.program_id(0) * pl.num_programs(1) + pl.program_id(1)) * 2 + hh
            thr = thr_ref[r, i]
            return lax.fori_loop(0, n_full, lambda j, n: n + (klast_ref[r, j] >= thr).astype(jnp.int32), 0)

        first = jnp.minimum(skippable(0), skippable(1))
    if wide > 1:
        def wide_body(j, states):
            off = pl.multiple_of((first + wide * j) * tk, tk)
            return tuple(consume(states[hh], hh, scores(hh, kf_ref, off, wide * tk), vf_ref, off, wide * tk)
                         for hh in heads)

        n_wide = (n_full - first) // wide
        init = lax.fori_loop(0, n_wide, wide_body, init)
        first = first + wide * n_wide
    states = tuple(cached_step(hh) for hh in heads) if split else lax.fori_loop(first, n_full, body, init)

    mask = lax.broadcasted_iota(jnp.int32, (tq, tq), 0) >= lax.broadcasted_iota(jnp.int32, (tq, tq), 1)
    outs = []
    for hh in heads:
        s = jnp.where(mask, scores(hh, kd_ref, doff, tq), NEG_BIG)
        _, acc = consume(states[hh], hh, s, vd_ref, doff, tq)
        outs.append(acc / acc[:, D_HEAD:D_HEAD + 1])
    lane = lax.broadcasted_iota(jnp.int32, (1, LANES), 1)
    o_ref[0] = jnp.where(lane < D_HEAD, outs[0], pltpu.roll(outs[1], D_HEAD, axis=1)).astype(BF16)


def _skip_tables(edges, tile, block, g_q, g_k):
    b = edges.shape[0]
    per = block // tile
    qk_bound = 1.02 * D_HEAD ** 0.5 * LOG2E * jnp.max(jnp.abs(g_q)) * jnp.max(jnp.abs(g_k))
    margin = 2.0 * qk_bound + (EXP2_UNDERFLOW + 1.0)
    q_first = edges[:, 0::per, 0, :N_HEADS]
    k_last = edges[:, per - 1::per, 1, :N_HEADS]
    to_rows = lambda a: jnp.transpose(a, (0, 2, 1)).reshape(b * N_HEADS, -1)
    return to_rows(q_first + margin), to_rows(k_last)


def _attention(qa, ka, va, skip_tables=None, cache=None):
    b, h, t_q, _ = qa.shape
    split = cache is not None
    kf, vf = cache[:2] if split else (ka, va)
    t_f = kf.shape[2]
    past = t_f if split else 0
    tq = min(ATT_BLOCK, t_q)
    tk = math.gcd(past, ATT_CACHE_BLOCK) if split else tq
    assert t_q % tq == 0 and past % tk == 0 and (tq % tk == 0 or t_q == tq)
    assert not split or (tk == past and t_q == tq)
    skip = skip_tables is not None
    if skip:
        assert not split and h == N_HEADS
        thr, k_last = skip_tables
    else:
        thr = k_last = jnp.zeros((1, 1), F32)
    kv_bytes = 2 * t_f * LANES * 2
    kv_mode = dict(pipeline_mode=pl.Buffered(1)) if 4 * kv_bytes > VMEM_LIMIT // 4 else {}
    kv_spec = pl.BlockSpec((1, 2, t_f, LANES), lambda i, p, j: (i, p, 0, 0), **kv_mode)
    v_spec = kv_spec
    if split:
        layer = cache[2]
        v_spec = pl.BlockSpec((None, 1, 2, D_HEAD, past), lambda i, p, j: (layer, i, p, 0, 0))
    q_spec = pl.BlockSpec((1, 2, tq, LANES), lambda i, p, j: (i, p, j, 0))
    smem = pl.BlockSpec(memory_space=pltpu.SMEM)
    own_specs, own_args = ([q_spec, q_spec], [ka, va]) if split else ([], [])
    return pl.pallas_call(
        functools.partial(_attn_kernel, tq=tq, tk=tk, past=past, skip=skip, split=split,
                          wide=1 if split else ATT_WIDE),
        grid=(b, h // 2, t_q // tq),
        in_specs=[smem, smem, q_spec, kv_spec, v_spec] + own_specs,
        out_specs=pl.BlockSpec((1, tq, LANES), lambda i, p, j: (i, j, p)),
        out_shape=jax.ShapeDtypeStruct((b, t_q, h * D_HEAD), BF16),
        compiler_params=pltpu.CompilerParams(dimension_semantics=("arbitrary", "arbitrary", "arbitrary"),
                                             vmem_limit_bytes=VMEM_LIMIT),
        name="fox_attention",
    )(thr, k_last, qa, kf, vf, *own_args)


def _gelu_tanh(x):
    return 0.5 * x * (1.0 + jnp.tanh(math.sqrt(2.0 / math.pi) * (x + 0.044715 * (x * x * x))))


def _long_conv(dbuf, phase, cw_ref, cd_ref, nb, tt, first):
    rows_kept = tt + first + K_LONG - 1 - SUBLANES
    for r in range(1, SUBLANES):
        phase[r - 1, :, :rows_kept, :] = dbuf[:, r:r + rows_kept, :]
    rb = CONV_ACC_ROWS // nb
    for cb in range(W_HALF // LANES):
        cols = slice(cb * LANES, (cb + 1) * LANES)
        w_cols = cw_ref[:, cols]
        for r0 in range(0, tt, rb):
            acc = None
            for k in range(K_LONG):
                r, a = (first + k) % SUBLANES, (first + k) // SUBLANES * SUBLANES
                src = dbuf if r == 0 else phase.at[r - 1]
                term = w_cols[k:k + 1] * src[:, a + r0:a + r0 + rb, cols]
                acc = term if acc is None else acc + term
            cd_ref[:, r0:r0 + rb, cols] = acc


def _odd_in_kernel(x_ref, hist_ref, gmix_ref, w_ref, gvc_ref, ws_ref, bs_ref, cw_ref, gd_ref,
                   oc_ref, od_ref, hout_ref, *rest, nb, tt, cs, emit_vc):
    vc_ref = rest[0] if emit_vc else None
    dbuf, phase, cdbuf = rest[-3:]
    tm = nb * tt
    pad = 4 * SUBLANES
    nh = K_LONG - 1

    @pl.when(pl.program_id(1) == 0)
    def _():
        dbuf[:, pad - nh:pad, :] = hist_ref[...]

    x = x_ref[...].reshape(tm, x_ref.shape[-1])
    xn = _rms(x, gmix_ref[...]).astype(BF16)

    def proj(i):
        return jnp.dot(xn, w_ref[:, i * W_HALF:(i + 1) * W_HALF], preferred_element_type=F32)

    u = _gelu_tanh(proj(0))
    vc = _rms(_gelu_tanh(proj(1)), gvc_ref[...])
    if emit_vc:
        vc_ref[...] = vc.reshape(nb, tt, W_HALF)
    vcb = vc.astype(BF16)

    row = lax.broadcasted_iota(jnp.int32, (cs, cs), 0)
    col = lax.broadcasted_iota(jnp.int32, (cs, cs), 1)
    per_b = tt // cs
    for g in range(N_GROUPS):
        wsg = jnp.where(row >= col, ws_ref[g], 0.0).astype(BF16)
        for ci in range(tm // cs):
            r0 = ci * cs
            gate = jnp.dot(wsg, vcb[r0:r0 + cs, g * LANES:(g + 1) * LANES], preferred_element_type=F32) + bs_ref[g]
            oc = u[r0:r0 + cs, g * LANES:(g + 1) * LANES] * gate
            t0 = (ci % per_b) * cs
            oc_ref[ci // per_b, t0:t0 + cs, g * LANES:(g + 1) * LANES] = oc.astype(BF16)

    glu = proj(2) * jax.nn.sigmoid(proj(3))
    dbuf[:, pad:pad + tt, :] = glu.reshape(nb, tt, W_HALF)
    _long_conv(dbuf, phase, cw_ref, cdbuf, nb, tt, pad - nh)
    y = _rms(cdbuf[...].reshape(tm, W_HALF), gd_ref[...])
    od_ref[...] = (y * jax.nn.sigmoid(y)).astype(BF16).reshape(nb, tt, W_HALF)
    tail = dbuf[:, pad + tt - nh:pad + tt, :]
    dbuf[:, pad - nh:pad, :] = tail
    hout_ref[...] = tail


def _odd_in(x, hist_d, gmix, w_in, gvc, ws, bs, conv_w, gd, emit_vc):
    b, t, d = x.shape
    assert t >= K_LONG - 1
    nb, tt = _tiles(b, t)
    cs = min(GMLP_CHUNK, tt)
    assert tt % cs == 0 and CONV_ACC_ROWS % nb == 0 and tt % (CONV_ACC_ROWS // nb) == 0
    grid = (b // nb, t // tt)
    row = lambda shape: pl.BlockSpec(shape, lambda i, j: (i, j, 0))
    per_b = lambda shape: pl.BlockSpec(shape, lambda i, j: (i, 0, 0))
    ws = ws[:, :cs, :cs]
    bs = bs[:, :cs, :]
    w_in, w_in_spec = _layer_weight(w_in)
    out_specs = [row((nb, tt, W_HALF)), row((nb, tt, W_HALF)), per_b((nb, K_LONG - 1, W_HALF))]
    out_shape = [jax.ShapeDtypeStruct((b, t, W_HALF), BF16), jax.ShapeDtypeStruct((b, t, W_HALF), BF16),
                 jax.ShapeDtypeStruct((b, K_LONG - 1, W_HALF), F32)]
    if emit_vc:
        out_specs.append(row((nb, tt, W_HALF)))
        out_shape.append(jax.ShapeDtypeStruct((b, t, W_HALF), F32))
    hist_rows = 4 * SUBLANES
    return pl.pallas_call(
        functools.partial(_odd_in_kernel, nb=nb, tt=tt, cs=cs, emit_vc=emit_vc),
        grid=grid,
        in_specs=[row((nb, tt, d)), per_b((nb, K_LONG - 1, W_HALF)),
                  _const_spec(gmix.shape), w_in_spec, _const_spec(gvc.shape),
                  _const_spec(ws.shape), _const_spec(bs.shape), _const_spec(conv_w.shape), _const_spec(gd.shape)],
        out_specs=tuple(out_specs),
        out_shape=tuple(out_shape),
        scratch_shapes=[pltpu.VMEM((nb, tt + hist_rows, W_HALF), F32),
                        pltpu.VMEM((SUBLANES - 1, nb, tt + hist_rows - SUBLANES, W_HALF), F32),
                        pltpu.VMEM((nb, tt, W_HALF), F32)],
        compiler_params=pltpu.CompilerParams(dimension_semantics=("arbitrary", "arbitrary"),
                                             vmem_limit_bytes=VMEM_LIMIT),
        name="odd_in",
    )(x, hist_d, gmix, w_in, gvc, ws, bs, conv_w, gd)


def _mix_ffn_kernel(x_ref, oa_ref, ob_ref, hist_ref, wo_ref, g_ref, wu_ref, cw_ref, wd_ref,
                    y_ref, hout_ref, hbuf, abuf, *, nb, tt):
    tm = nb * tt
    pad = SUBLANES
    d_ff = wd_ref.shape[0]

    @pl.when(pl.program_id(1) == 0)
    def _():
        hbuf[:, pad - (K_SHORT - 1):pad, :] = hist_ref[...]

    d = x_ref.shape[-1]
    oa = oa_ref[...].reshape(tm, W_HALF)
    ob = ob_ref[...].reshape(tm, W_HALF)
    x1 = (x_ref[...].reshape(tm, d)
          + jnp.dot(oa, wo_ref[:W_HALF, :], preferred_element_type=F32)
          + jnp.dot(ob, wo_ref[W_HALF:, :], preferred_element_type=F32))
    xn = _rms(x1, g_ref[...]).astype(BF16)

    def conv_up(lo):
        h = jnp.dot(xn, wu_ref[:, lo:lo + d_ff], preferred_element_type=F32)
        hbuf[:, pad:pad + tt, lo:lo + d_ff] = h.reshape(nb, tt, d_ff)
        cw = cw_ref[:, lo:lo + d_ff]
        return (cw[2:3] * hbuf[:, pad:pad + tt, lo:lo + d_ff]
                + cw[1:2] * hbuf[:, pad - 1:pad - 1 + tt, lo:lo + d_ff]
                + cw[0:1] * hbuf[:, pad - 2:pad - 2 + tt, lo:lo + d_ff]).reshape(tm, d_ff)

    gate = conv_up(0)
    val = conv_up(d_ff)
    abuf[...] = (gate * jax.nn.sigmoid(gate) * val).astype(BF16)
    y_ref[...] = (x1 + jnp.dot(abuf[...], wd_ref[...], preferred_element_type=F32)).reshape(nb, tt, d)
    tail = hbuf[:, pad + tt - (K_SHORT - 1):pad + tt, :]
    hbuf[:, pad - (K_SHORT - 1):pad, :] = tail
    hout_ref[...] = tail


def _mix_ffn(x, oa, ob, hist_ffn, w_out, g_ffn, w_up, conv_w, w_down):
    b, t, d = x.shape
    assert t >= K_SHORT - 1
    nb, tt = _tiles(b, t)
    grid = (b // nb, t // tt)
    (w_out, w_out_spec), (w_up, w_up_spec), (w_down, w_down_spec) = map(_layer_weight, (w_out, w_up, w_down))
    two_ff = w_up.shape[-1]
    row = lambda shape: pl.BlockSpec(shape, lambda i, j: (i, j, 0))
    per_b = lambda shape: pl.BlockSpec(shape, lambda i, j: (i, 0, 0))
    return pl.pallas_call(
        functools.partial(_mix_ffn_kernel, nb=nb, tt=tt),
        grid=grid,
        in_specs=[row((nb, tt, d)), row((nb, tt, W_HALF)), row((nb, tt, W_HALF)), per_b((nb, K_SHORT - 1, two_ff)),
                  w_out_spec, _const_spec(g_ffn.shape), w_up_spec, _const_spec(conv_w.shape), w_down_spec],
        out_specs=(row((nb, tt, d)), per_b((nb, K_SHORT - 1, two_ff))),
        out_shape=(jax.ShapeDtypeStruct((b, t, d), F32),
                   jax.ShapeDtypeStruct((b, K_SHORT - 1, two_ff), F32)),
        scratch_shapes=[pltpu.VMEM((nb, tt + SUBLANES, two_ff), F32), pltpu.VMEM((nb * tt, two_ff // 2), BF16)],
        compiler_params=pltpu.CompilerParams(dimension_semantics=("arbitrary", "arbitrary"),
                                             vmem_limit_bytes=VMEM_LIMIT),
        name="mix_ffn",
    )(x, oa, ob, hist_ffn, w_out, g_ffn, w_up, conv_w, w_down)


def _prep_params(g_mix, w_in_even, b_f, g_q, g_k, conv_b, w_out_even, w_in_odd, g_vc, w_s, b_s,
                 conv_d, g_d, w_out_odd, g_ffn, w_up, conv_ffn, w_down):
    depth = g_mix.shape[0]
    off_f = 3 * W_HALF
    head_id = jnp.arange(W_HALF) // D_HEAD
    hmat = jnp.where(head_id[:, None] == head_id[None, :], 1.0 / D_HEAD, 0.0).astype(BF16)
    w_up, w_down, w_in_odd, w_out_even, w_out_odd = (
        w.astype(BF16) for w in (w_up, w_down, w_in_odd, w_out_even, w_out_odd))
    layers = []
    for l in range(depth):
        i = l // 2
        p = dict(g_mix=g_mix[l][None], g_ffn=g_ffn[l][None], w_up=(w_up, l),
                 conv_ffn=conv_ffn[l], w_down=(w_down, l))
        if l % 2 == 0:
            w = w_in_even[i]
            p.update(
                w_main=jnp.concatenate([w[:, :off_f], w[:, off_f + N_HEADS:]], axis=1).astype(BF16),
                w_f=jnp.pad(w[:, off_f:off_f + N_HEADS], ((0, 0), (0, LANES - N_HEADS))).astype(BF16),
                b_f=jnp.pad(b_f[i], (0, LANES - N_HEADS))[None],
                g_q=jnp.tile(g_q[i], N_HEADS)[None], g_k=jnp.tile(g_k[i], N_HEADS)[None],
                conv_w=conv_b[i], w_out=(w_out_even, i), hmat=hmat)
        else:
            p.update(
                w_in=(w_in_odd, i), g_vc=g_vc[i][None], w_s=w_s[i],
                b_s=jnp.broadcast_to(b_s[i][:, :, None], b_s[i].shape + (LANES,)),
                conv_w=conv_d[i], g_d=g_d[i][None], w_out=(w_out_odd, i))
        layers.append(p)
    return layers


def _trunk(x, layers, hist_b, hist_d, hist_ffn, att_cache):
    b = x.shape[0]
    ks, vs, lfs, hbs, vcs, hds, hfs = [], [], [], [], [], [], []
    kv_stack = None
    if att_cache is not None:
        cache_k = jnp.transpose(att_cache[0], (0, 1, 3, 4, 2))
        cache_v = jnp.transpose(att_cache[1], (0, 1, 3, 4, 2))
        cache_lf = jnp.transpose(att_cache[2], (0, 1, 3, 2))
    for l, p in enumerate(layers):
        i = l // 2
        if l % 2 == 0:
            if att_cache is None:
                c0 = jnp.zeros((b, 1, LANES), F32)
            else:
                ka_c, c0 = _cache_prep(cache_k, cache_lf, i)
            qa, ka, va, k, v, lf, ob, hb, edges = _even_in(x, hist_b[i], c0, p['g_mix'], p['w_main'], p['w_f'],
                                                           p['b_f'], p['g_q'], p['g_k'], p['conv_w'], p['hmat'],
                                                           prev_kv=kv_stack)
            t = x.shape[1]
            if att_cache is not None:
                oa = _attention(qa, ka, va, cache=(ka_c, cache_v, i))
            else:
                tables = _skip_tables(edges, t // edges.shape[1], min(ATT_BLOCK, t), p['g_q'], p['g_k'])
                oa = _attention(qa, ka, va, tables)
            if k.ndim == 5:
                kv_stack = (k, v)
            else:
                ks.append(k.reshape(b, t, N_HEADS, D_HEAD))
                vs.append(v.reshape(b, t, N_HEADS, D_HEAD))
            lfs.append(lf)
            hbs.append(hb)
        else:
            oa, ob, hd, *vc = _odd_in(x, hist_d[i], p['g_mix'], p['w_in'], p['g_vc'], p['w_s'], p['b_s'],
                                      p['conv_w'], p['g_d'], emit_vc=att_cache is not None)
            vcs.extend(vc)
            hds.append(hd)
        x, hf = _mix_ffn(x, oa, ob, hist_ffn[l], p['w_out'], p['g_ffn'], p['w_up'], p['conv_ffn'], p['w_down'])
        hfs.append(hf)
    if kv_stack is not None:
        all_k, all_v = (jnp.transpose(a, (0, 1, 4, 2, 3)) for a in kv_stack)
    else:
        all_k, all_v = jnp.stack(ks), jnp.stack(vs)
    return (x, all_k, all_v, jnp.stack(lfs), jnp.stack(hbs),
            jnp.stack(vcs) if vcs else None, jnp.stack(hds), jnp.stack(hfs))


def kernel(x_prompt, x_sample, cache_k, cache_v, cache_logf, state_conv_b, state_conv_d, state_conv_ffn,
           g_mix, w_in_even, b_f, g_q, g_k, conv_b, w_out_even, w_in_odd, g_vc, w_s, b_s,
           conv_d, g_d, w_out_odd, g_ffn, w_up, conv_ffn, w_down):
    layers = _prep_params(g_mix, w_in_even, b_f, g_q, g_k, conv_b, w_out_even, w_in_odd, g_vc, w_s, b_s,
                          conv_d, g_d, w_out_odd, g_ffn, w_up, conv_ffn, w_down)
    b = x_prompt.shape[0]
    depth = g_mix.shape[0]
    n_even, n_odd = (depth + 1) // 2, depth // 2
    zb = jnp.zeros((n_even, b, K_SHORT - 1, W_HALF), F32)
    zd = jnp.zeros((n_odd, b, K_LONG - 1, W_HALF), F32)
    zf = jnp.zeros((depth, b, K_SHORT - 1, w_up.shape[-1]), F32)
    (y_prompt, p_k, p_v, p_logf, p_conv_b, _, p_conv_d, p_conv_ffn) = _trunk(
        x_prompt, layers, zb, zd, zf, None)
    (y_sample, s_k, s_v, s_logf, s_conv_b, s_vc, s_conv_d, s_conv_ffn) = _trunk(
        x_sample, layers, state_conv_b, state_conv_d, state_conv_ffn, (cache_k, cache_v, cache_logf))
    return (y_prompt, y_sample, p_k, p_v, p_logf, p_conv_b, p_conv_d, p_conv_ffn,
            s_k, s_v, s_logf, s_conv_b, s_vc, s_conv_d, s_conv_ffn)
```
